```python
import jax, jax.numpy as jnp
from jax import lax
import numpy as np

D_MODEL = 1024
BATCH = 2
SEQ = 8192
DEPTH = 2

D_MIX = D_MODEL
HEAD_DIM = 64
ATTN_WIDTH = D_MIX // 2
N_HEADS = ATTN_WIDTH // HEAD_DIM
N_KV_HEADS = 2
GROUP = N_HEADS // N_KV_HEADS
KV_WIDTH = N_KV_HEADS * HEAD_DIM
CONV_CHANNELS = D_MIX - ATTN_WIDTH
CONV_WIDTH = 31
WINDOW = 128
BLOCK = 128
ROPE_THETA = 10000.0
D_FF = ((8 * D_MODEL // 3 + 127) // 128) * 128
D_IN = ATTN_WIDTH + 2 * KV_WIDTH + 2 * CONV_CHANNELS
EPS = 1e-5

kernel_name = "hybrid_swa_sink_conformer_conv_macaron"


def rms_norm(x, g):
    xf = x.astype(jnp.float32)
    y = xf * lax.rsqrt(jnp.mean(xf * xf, axis=-1, keepdims=True) + EPS)
    return (y * g.astype(jnp.float32)).astype(x.dtype)


def layer_norm(x, g, b):
    xf = x.astype(jnp.float32)
    mu = jnp.mean(xf, axis=-1, keepdims=True)
    xc = xf - mu
    y = xc * lax.rsqrt(jnp.mean(xc * xc, axis=-1, keepdims=True) + EPS)
    return (y * g.astype(jnp.float32) + b.astype(jnp.float32)).astype(x.dtype)


def swiglu(h, w_gate, w_up, w_down):
    return (jax.nn.silu(h @ w_gate) * (h @ w_up)) @ w_down


def rope_tables(positions):
    inv_freq = 1.0 / (ROPE_THETA ** (jnp.arange(0, HEAD_DIM, 2, dtype=jnp.float32) / HEAD_DIM))
    ang = positions.astype(jnp.float32)[..., None] * inv_freq
    return jnp.cos(ang), jnp.sin(ang)


def apply_rope(t, cos, sin):
    tf = t.astype(jnp.float32)
    t1, t2 = jnp.split(tf, 2, axis=-1)
    c = cos[:, :, None, :]
    s = sin[:, :, None, :]
    return jnp.concatenate([t1 * c - t2 * s, t2 * c + t1 * s], axis=-1).astype(t.dtype)


def sliding_window_attention(q, k, v, sinks):
    B, S = q.shape[0], q.shape[1]
    nb = S // BLOCK
    qb = q.reshape(B, nb, BLOCK, N_KV_HEADS, GROUP, HEAD_DIM).astype(jnp.float32)

    def band(t):
        tb = t.reshape(B, nb, BLOCK, N_KV_HEADS, HEAD_DIM)
        prev = jnp.pad(tb[:, :-1], ((0, 0), (1, 0), (0, 0), (0, 0), (0, 0)))
        return jnp.concatenate([prev, tb], axis=2).astype(jnp.float32)

    kb, vb = band(k), band(v)
    scores = jnp.einsum('bnqkgd,bnjkd->bnkgqj', qb, kb) * (HEAD_DIM ** -0.5)

    q_local = jnp.arange(BLOCK)[:, None] + BLOCK
    k_local = jnp.arange(2 * BLOCK)[None, :]
    rel = q_local - k_local
    in_window = (rel >= 0) & (rel < WINDOW)
    block_valid = (jnp.arange(nb)[:, None] > 0) | (k_local >= BLOCK)
    mask = in_window[None, :, :] & block_valid[:, None, :]
    neg = jnp.finfo(jnp.float32).min
    scores = jnp.where(mask[None, :, None, None, :, :], scores, neg)

    sink = sinks.astype(jnp.float32).reshape(N_KV_HEADS, GROUP)[None, None, :, :, None, None]
    m = jnp.maximum(jnp.max(scores, axis=-1, keepdims=True), sink)
    p = jnp.exp(scores - m)
    denom = jnp.sum(p, axis=-1, keepdims=True) + jnp.exp(sink - m)
    probs = p / denom
    out = jnp.einsum('bnkgqj,bnjkd->bnqkgd', probs, vb)
    return out.reshape(B, S, N_HEADS * HEAD_DIM).astype(q.dtype)


def conformer_conv(u, conv_w, conv_b, ln_g, ln_b):
    a, gate = jnp.split(u, 2, axis=-1)
    h = a * jax.nn.sigmoid(gate)
    h = lax.conv_general_dilated(
        h, conv_w[:, None, :].astype(h.dtype),
        window_strides=(1,), padding=[(CONV_WIDTH - 1, 0)],
        dimension_numbers=('NWC', 'WIO', 'NWC'),
        feature_group_count=CONV_CHANNELS) + conv_b
    h = layer_norm(h, ln_g, ln_b)
    return jax.nn.silu(h)


def setup_inputs(seed: int = 0) -> dict:
    key = jax.random.key(seed)
    ks = jax.random.split(key, 20)
    f32 = jnp.float32

    def w(k, shape, fan_in):
        return jax.random.normal(k, shape, f32) * (fan_in ** -0.5)

    def gain(k, shape):
        return 1.0 + 0.05 * jax.random.normal(k, shape, f32)

    x = jax.random.normal(ks[0], (BATCH, SEQ, D_MODEL), f32)
    positions = jnp.broadcast_to(jnp.arange(SEQ, dtype=jnp.int32), (BATCH, SEQ))
    return {
        "x": x,
        "positions": positions,
        "ffn1_norm": gain(ks[1], (DEPTH, D_MODEL)),
        "ffn1_w_gate": w(ks[2], (DEPTH, D_MODEL, D_FF), D_MODEL),
        "ffn1_w_up": w(ks[3], (DEPTH, D_MODEL, D_FF), D_MODEL),
        "ffn1_w_down": w(ks[4], (DEPTH, D_FF, D_MODEL), D_FF),
        "mix_norm": gain(ks[5], (DEPTH, D_MODEL)),
        "w_in": w(ks[6], (DEPTH, D_MODEL, D_IN), D_MODEL),
        "conv_w": w(ks[7], (DEPTH, CONV_WIDTH, CONV_CHANNELS), CONV_WIDTH),
        "conv_b": 0.02 * jax.random.normal(ks[8], (DEPTH, CONV_CHANNELS), f32),
        "conv_ln_g": gain(ks[9], (DEPTH, CONV_CHANNELS)),
        "conv_ln_b": 0.02 * jax.random.normal(ks[10], (DEPTH, CONV_CHANNELS), f32),
        "attn_sinks": 0.5 * jax.random.normal(ks[11], (DEPTH, N_HEADS), f32),
        "w_out": w(ks[12], (DEPTH, D_MIX, D_MODEL), D_MIX),
        "ffn2_norm": gain(ks[13], (DEPTH, D_MODEL)),
        "ffn2_w_gate": w(ks[14], (DEPTH, D_MODEL, D_FF), D_MODEL),
        "ffn2_w_up": w(ks[15], (DEPTH, D_MODEL, D_FF), D_MODEL),
        "ffn2_w_down": w(ks[16], (DEPTH, D_FF, D_MODEL), D_FF),
        "final_norm": gain(ks[17], (D_MODEL,)),
    }


def reference(x, positions, ffn1_norm, ffn1_w_gate, ffn1_w_up, ffn1_w_down,
              mix_norm, w_in, conv_w, conv_b, conv_ln_g, conv_ln_b, attn_sinks, w_out,
              ffn2_norm, ffn2_w_gate, ffn2_w_up, ffn2_w_down, final_norm):
    B, S = x.shape[0], x.shape[1]
    cos, sin = rope_tables(positions)
    q_end = ATTN_WIDTH
    k_end = q_end + KV_WIDTH
    v_end = k_end + KV_WIDTH
    for l in range(DEPTH):
        x = x + 0.5 * swiglu(rms_norm(x, ffn1_norm[l]), ffn1_w_gate[l], ffn1_w_up[l], ffn1_w_down[l])
        h = rms_norm(x, mix_norm[l])
        p = h @ w_in[l]
        q = apply_rope(p[..., :q_end].reshape(B, S, N_HEADS, HEAD_DIM), cos, sin)
        k = apply_rope(p[..., q_end:k_end].reshape(B, S, N_KV_HEADS, HEAD_DIM), cos, sin)
        v = p[..., k_end:v_end].reshape(B, S, N_KV_HEADS, HEAD_DIM)
        u = p[..., v_end:]
        attn_out = sliding_window_attention(q, k, v, attn_sinks[l])
        conv_out = conformer_conv(u, conv_w[l], conv_b[l], conv_ln_g[l], conv_ln_b[l])
        x = x + jnp.concatenate([attn_out, conv_out], axis=-1) @ w_out[l]
        x = x + 0.5 * swiglu(rms_norm(x, ffn2_norm[l]), ffn2_w_gate[l], ffn2_w_up[l], ffn2_w_down[l])
    return rms_norm(x, final_norm)
```

```python
import functools

import jax
import jax.numpy as jnp
from jax import lax
from jax.experimental import pallas as pl
from jax.experimental.pallas import tpu as pltpu

HEAD_DIM = 64
N_HEADS = 8
N_KV_HEADS = 2
ATTN_WIDTH = N_HEADS * HEAD_DIM
KV_WIDTH = N_KV_HEADS * HEAD_DIM
CONV_WIDTH = 31
BLOCK = 128
ROPE_THETA = 10000.0
EPS = 1e-5

LANES = 128
SUBLANES = 8
MXU_COLS = 256
VMEM_LIMIT_BYTES = 56 * 1024 * 1024

CONV_HIST = 32
HALF = HEAD_DIM // 2

BF16 = jnp.bfloat16
F32 = jnp.float32


def _sigmoid(x):
    return 1.0 / (1.0 + jnp.exp(-x))


def _rms_norm(x, g):
    return x * lax.rsqrt(jnp.mean(x * x, axis=-1, keepdims=True) + EPS) * g


def _rope_table_kernel(pos_ref, invf_ref, cos_ref, sin_ref):
    ang = pos_ref[...].astype(F32) * invf_ref[...]
    cos_ref[...] = jnp.cos(ang)
    sin_ref[...] = jnp.sin(ang)


def _rope_tables(positions):
    B, S = positions.shape
    n = B * S
    per_row = LANES // HALF
    inv_freq = 1.0 / (ROPE_THETA ** (jnp.arange(0, HEAD_DIM, 2, dtype=F32) / HEAD_DIM))
    invf = jnp.tile(inv_freq, per_row).reshape(1, LANES)
    pos = jnp.repeat(positions.reshape(n // per_row, per_row), HALF, axis=1)
    rows = n // per_row
    tr = 1024
    cos_c, sin_c = pl.pallas_call(
        _rope_table_kernel,
        grid=(rows // tr,),
        in_specs=[pl.BlockSpec((tr, LANES), lambda i: (i, 0)),
                  pl.BlockSpec((1, LANES), lambda i: (0, 0))],
        out_specs=[pl.BlockSpec((tr, LANES), lambda i: (i, 0)),
                   pl.BlockSpec((tr, LANES), lambda i: (i, 0))],
        out_shape=[jax.ShapeDtypeStruct((rows, LANES), F32)] * 2,
        name="rope_tables",
    )(pos, invf)
    cos = jnp.tile(cos_c.reshape(n, HALF), (1, per_row)).reshape(B, S, LANES)
    sin = jnp.tile(sin_c.reshape(n, HALF), (1, per_row)).reshape(B, S, LANES)
    return cos, sin


def _ffn_kernel(x_ref, g_ref, wg_ref, wu_ref, wd_ref, fg_ref, o_ref, a_ref, *, chunks, final_norm):
    x = x_ref[...]
    h = _rms_norm(x, g_ref[...]).astype(BF16)
    for lo, width in chunks:
        gate = jnp.dot(h, wg_ref[:, lo:lo + width], preferred_element_type=F32)
        up = jnp.dot(h, wu_ref[:, lo:lo + width], preferred_element_type=F32)
        a_ref[:, lo:lo + width] = (gate * _sigmoid(gate) * up).astype(BF16)
    y = x + 0.5 * jnp.dot(a_ref[...], wd_ref[...], preferred_element_type=F32)
    if final_norm:
        y = _rms_norm(y, fg_ref[...])
    o_ref[...] = y


def _ff_chunks(d_ff, chunk):
    out, lo = [], 0
    while lo < d_ff:
        w = min(chunk, d_ff - lo)
        out.append((lo, w))
        lo += w
    return tuple(out)


def _ffn(x, norm_g, w_gate, w_up, w_down, final_g, *, final_norm, tm=512, ff_chunk=512):
    n, d = x.shape
    d_ff = w_gate.shape[1]
    const = lambda i: (0, 0)
    resident = functools.partial(pl.BlockSpec, index_map=const, pipeline_mode=pl.Buffered(1))
    return pl.pallas_call(
        functools.partial(_ffn_kernel, chunks=_ff_chunks(d_ff, ff_chunk), final_norm=final_norm),
        grid=(n // tm,),
        in_specs=[pl.BlockSpec((tm, d), lambda i: (i, 0)),
                  resident((1, d)),
                  resident((d, d_ff)),
                  resident((d, d_ff)),
                  resident((d_ff, d)),
                  resident((1, d))],
        out_specs=pl.BlockSpec((tm, d), lambda i: (i, 0)),
        out_shape=jax.ShapeDtypeStruct((n, d), F32),
        scratch_shapes=[pltpu.VMEM((tm, d_ff), BF16)],
        compiler_params=pltpu.CompilerParams(
            dimension_semantics=("arbitrary",), vmem_limit_bytes=VMEM_LIMIT_BYTES),
        name="ffn",
    )(x, norm_g.reshape(1, d), w_gate, w_up, w_down, final_g.reshape(1, d))


def _mixer_kernel(sink_ref, x_ref, cos_ref, sin_ref, g_ref, win_ref, cw_ref, cb_ref, lng_ref, lnb_ref,
                  wout_ref, o_ref, kv_ref, hg_ref, *, tm):
    s_idx = pl.program_id(1)
    conv_c = cw_ref.shape[1]

    @pl.when(s_idx == 0)
    def _():
        kv_ref[:, 0:BLOCK, :] = jnp.zeros((8, BLOCK, LANES), BF16)
        hg_ref[0:CONV_HIST, :] = jnp.zeros((CONV_HIST, conv_c), F32)

    x = x_ref[...]
    h = _rms_norm(x, g_ref[...]).astype(BF16)
    p = jnp.dot(h, win_ref[...], preferred_element_type=F32)

    lane = lax.broadcasted_iota(jnp.int32, (1, LANES), 1)
    cos = cos_ref[...]
    sin = jnp.where(lane < HEAD_DIM, -sin_ref[...], sin_ref[...])

    def rope(t):
        return t * cos + pltpu.roll(t, HEAD_DIM, 1) * sin

    q_end, k_end, v_end = ATTN_WIDTH, ATTN_WIDTH + KV_WIDTH, ATTN_WIDTH + 2 * KV_WIDTH
    q = [rope(p[:, c * LANES:(c + 1) * LANES]).astype(BF16) for c in range(ATTN_WIDTH // LANES)]
    k = rope(p[:, q_end:k_end])
    v = p[:, k_end:v_end]

    on_a = (lane & (HEAD_DIM - 1)) < HALF
    zero = jnp.zeros((), F32)
    rows = pl.ds(BLOCK, tm)
    kv_ref[0, rows, :] = jnp.where(on_a, k, zero).astype(BF16)
    kv_ref[1, rows, :] = jnp.where(on_a, zero, pltpu.roll(k, HALF, 1)).astype(BF16)
    kv_ref[2, rows, :] = jnp.where(on_a, pltpu.roll(k, LANES - HALF, 1), zero).astype(BF16)
    kv_ref[3, rows, :] = jnp.where(on_a, zero, k).astype(BF16)
    first = lane < HEAD_DIM
    v_sw = pltpu.roll(v, HEAD_DIM, 1)
    kv_ref[4, rows, :] = jnp.where(first, v, zero).astype(BF16)
    kv_ref[5, rows, :] = jnp.where(first, zero, v_sw).astype(BF16)
    kv_ref[6, rows, :] = jnp.where(first, v_sw, zero).astype(BF16)
    kv_ref[7, rows, :] = jnp.where(first, zero, v).astype(BF16)

    qi = lax.broadcasted_iota(jnp.int32, (BLOCK, 2 * BLOCK), 0)
    kj = lax.broadcasted_iota(jnp.int32, (BLOCK, 2 * BLOCK), 1)
    in_window = (kj > qi) & (kj <= qi + BLOCK)
    first_block_mask = in_window & ((kj >= BLOCK) | (s_idx > 0))
    neg = jnp.finfo(F32).min
    nt = (((1,), (1,)), ((), ()))

    def softmax_parts(s, mask, sink):
        s = jnp.where(mask, s, neg)
        m = jnp.maximum(jnp.max(s, axis=-1, keepdims=True), sink)
        e = jnp.exp(s - m)
        denom = jnp.sum(e, axis=-1, keepdims=True) + jnp.exp(sink - m)
        return e.astype(BF16), 1.0 / denom

    attn_cols = []
    for c in range(ATTN_WIDTH // LANES):
        j = c // (N_HEADS // N_KV_HEADS // 2)
        blocks = []
        for b in range(tm // BLOCK):
            mask = first_block_mask if b == 0 else in_window
            band = pl.ds(b * BLOCK, 2 * BLOCK)
            qb = q[c][b * BLOCK:(b + 1) * BLOCK]
            s_a = lax.dot_general(qb, kv_ref[2 * j, band, :], nt, preferred_element_type=F32)
            s_b = lax.dot_general(qb, kv_ref[2 * j + 1, band, :], nt, preferred_element_type=F32)
            p_a, r_a = softmax_parts(s_a, mask, sink_ref[2 * c])
            p_b, r_b = softmax_parts(s_b, mask, sink_ref[2 * c + 1])
            o_a = jnp.dot(p_a, kv_ref[4 + 2 * j, band, :], preferred_element_type=F32)
            o_b = jnp.dot(p_b, kv_ref[5 + 2 * j, band, :], preferred_element_type=F32)
            blocks.append(o_a * r_a + o_b * r_b)
        attn_cols.append(jnp.concatenate(blocks, axis=0))
    attn_out = jnp.concatenate(attn_cols, axis=1).astype(BF16)

    a = p[:, v_end:v_end + conv_c]
    gate = p[:, v_end + conv_c:v_end + 2 * conv_c]
    hg_ref[pl.ds(CONV_HIST, tm), :] = a * _sigmoid(gate)
    acc = jnp.broadcast_to(cb_ref[...], (tm, conv_c))
    for w in range(CONV_WIDTH):
        acc = acc + hg_ref[pl.ds(CONV_HIST - (CONV_WIDTH - 1) + w, tm), :] * cw_ref[w:w + 1, :]
    mu = jnp.mean(acc, axis=-1, keepdims=True)
    xc = acc - mu
    y = xc * lax.rsqrt(jnp.mean(xc * xc, axis=-1, keepdims=True) + EPS) * lng_ref[...] + lnb_ref[...]
    conv_out = (y * _sigmoid(y)).astype(BF16)

    mixed = jnp.concatenate([attn_out, conv_out], axis=1)
    o_ref[...] = x + jnp.dot(mixed, wout_ref[...], preferred_element_type=F32)

    kv_ref[:, 0:BLOCK, :] = kv_ref[:, tm:tm + BLOCK, :]
    hg_ref[0:CONV_HIST, :] = hg_ref[tm:tm + CONV_HIST, :]


def _mixer(x, cos, sin, norm_g, w_in, conv_w, conv_b, ln_g, ln_b, sinks, w_out, *, tm=512):
    B, S, d = x.shape
    d_in = w_in.shape[1]
    conv_c = conv_w.shape[1]
    d_mix = w_out.shape[0]
    const = lambda b, s: (0, 0)
    resident = functools.partial(pl.BlockSpec, index_map=const, pipeline_mode=pl.Buffered(1))
    tile = lambda width: pl.BlockSpec((None, tm, width), lambda b, s: (b, s, 0))
    return pl.pallas_call(
        functools.partial(_mixer_kernel, tm=tm),
        grid=(B, S // tm),
        in_specs=[pl.BlockSpec(memory_space=pltpu.SMEM),
                  tile(d), tile(LANES), tile(LANES),
                  resident((1, d)),
                  resident((d, d_in)),
                  resident((CONV_WIDTH, conv_c)),
                  resident((1, conv_c)),
                  resident((1, conv_c)),
                  resident((1, conv_c)),
                  resident((d_mix, d))],
        out_specs=tile(d),
        out_shape=jax.ShapeDtypeStruct((B, S, d), F32),
        scratch_shapes=[pltpu.VMEM((8, BLOCK + tm, LANES), BF16),
                        pltpu.VMEM((CONV_HIST + tm, conv_c), F32)],
        compiler_params=pltpu.CompilerParams(
            dimension_semantics=("arbitrary", "arbitrary"), vmem_limit_bytes=VMEM_LIMIT_BYTES),
        name="mixer",
    )(sinks, x, cos, sin, norm_g.reshape(1, d), w_in, conv_w, conv_b.reshape(1, conv_c),
      ln_g.reshape(1, conv_c), ln_b.reshape(1, conv_c), w_out)


def _prep_w_in(w_in):
    pair = jnp.arange(LANES).reshape(2, 2, HALF).transpose(1, 0, 2).reshape(LANES)
    n_rope = (ATTN_WIDTH + KV_WIDTH) // LANES
    perm = (jnp.arange(n_rope)[:, None] * LANES + pair[None, :]).reshape(-1)
    cols = jnp.concatenate([perm, jnp.arange(ATTN_WIDTH + KV_WIDTH, w_in.shape[1])])
    scale = jnp.where(jnp.arange(w_in.shape[1]) < ATTN_WIDTH, HEAD_DIM ** -0.5, 1.0).astype(F32)
    return (w_in[:, cols] * scale).astype(BF16)


def kernel(x, positions, ffn1_norm, ffn1_w_gate, ffn1_w_up, ffn1_w_down, mix_norm, w_in, conv_w, conv_b,
           conv_ln_g, conv_ln_b, attn_sinks, w_out, ffn2_norm, ffn2_w_gate, ffn2_w_up, ffn2_w_down,
           final_norm):
    B, S, d = x.shape
    depth = w_in.shape[0]
    cos, sin = _rope_tables(positions)
    x = x.reshape(B * S, d)
    for l in range(depth):
        x = _ffn(x, ffn1_norm[l], ffn1_w_gate[l].astype(BF16), ffn1_w_up[l].astype(BF16),
                 ffn1_w_down[l].astype(BF16), final_norm, final_norm=False)
        x = _mixer(x.reshape(B, S, d), cos, sin, mix_norm[l], _prep_w_in(w_in[l]), conv_w[l], conv_b[l],
                   conv_ln_g[l], conv_ln_b[l], attn_sinks[l], w_out[l].astype(BF16)).reshape(B * S, d)
        x = _ffn(x, ffn2_norm[l], ffn2_w_gate[l].astype(BF16), ffn2_w_up[l].astype(BF16),
                 ffn2_w_down[l].astype(BF16), final_norm, final_norm=(l == depth - 1))
    return x.reshape(B, S, d)
```

```python
import functools

import jax
import jax.numpy as jnp
from jax import lax
from jax.experimental import pallas as pl
from jax.experimental.pallas import tpu as pltpu

HEAD_DIM = 64
N_HEADS = 8
N_KV_HEADS = 2
ATTN_WIDTH = N_HEADS * HEAD_DIM
KV_WIDTH = N_KV_HEADS * HEAD_DIM
CONV_WIDTH = 31
BLOCK = 128
ROPE_THETA = 10000.0
EPS = 1e-5

LANES = 128
SUBLANES = 8
VMEM_LIMIT_BYTES = 56 * 1024 * 1024

CONV_HIST = 32
HALF = HEAD_DIM // 2

BF16 = jnp.bfloat16
F32 = jnp.float32


def _sigmoid(x):
    return 1.0 / (1.0 + jnp.exp(-x))


def _rms_norm(x, g):
    return x * lax.rsqrt(jnp.mean(x * x, axis=-1, keepdims=True) + EPS) * g


def _layer_spec(layer, *shape):
    zeros = (0,) * len(shape)
    return pl.BlockSpec((None,) + shape, lambda *_: (layer,) + zeros, pipeline_mode=pl.Buffered(1))


def _rope_table_kernel(pos_ref, invf_ref, cos_ref, sin_ref):
    ang = pos_ref[...].astype(F32) * invf_ref[...]
    lane = lax.broadcasted_iota(jnp.int32, (1, LANES), 1)
    sign = jnp.where(lane < HEAD_DIM, -1.0, 1.0).astype(F32)
    for table, out_ref, scale in ((jnp.cos(ang), cos_ref, None), (jnp.sin(ang), sin_ref, sign)):
        for i in range(LANES // HALF):
            t = jnp.where((lane >= i * HALF) & (lane < (i + 1) * HALF), table, 0.0)
            t = t + pltpu.roll(t, 2 * HALF, 1)
            t = t + pltpu.roll(t, HALF, 1)
            out_ref[i] = t if scale is None else t * scale


def _rope_tables(positions):
    B, S = positions.shape
    n = B * S
    groups = LANES // HALF
    rows = n // groups
    inv_freq = 1.0 / (ROPE_THETA ** (jnp.arange(0, HEAD_DIM, 2, dtype=F32) / HEAD_DIM))
    invf = jnp.tile(inv_freq, groups).reshape(1, LANES)
    pos = jnp.repeat(positions.reshape(groups, rows).T, HALF, axis=1)
    tr = 512
    spec_out = pl.BlockSpec((groups, tr, LANES), lambda i: (0, i, 0))
    cos, sin = pl.pallas_call(
        _rope_table_kernel,
        grid=(rows // tr,),
        in_specs=[pl.BlockSpec((tr, LANES), lambda i: (i, 0)),
                  pl.BlockSpec((1, LANES), lambda i: (0, 0))],
        out_specs=[spec_out, spec_out],
        out_shape=[jax.ShapeDtypeStruct((groups, rows, LANES), F32)] * 2,
        name="rope_tables",
    )(pos, invf)
    return cos.reshape(B, S, LANES), sin.reshape(B, S, LANES)


def _ffn_kernel(x_ref, g_ref, wg_ref, wu_ref, wd_ref, fg_ref, o_ref, a_ref, *, chunks, final_norm):
    x = x_ref[...]
    h = _rms_norm(x, g_ref[...]).astype(BF16)
    for lo, width in chunks:
        gate = jnp.dot(h, wg_ref[:, lo:lo + width], preferred_element_type=F32)
        up = jnp.dot(h, wu_ref[:, lo:lo + width], preferred_element_type=F32)
        a_ref[:, lo:lo + width] = (gate * _sigmoid(gate) * up).astype(BF16)
    y = x + 0.5 * jnp.dot(a_ref[...], wd_ref[...], preferred_element_type=F32)
    if final_norm:
        y = _rms_norm(y, fg_ref[...])
    o_ref[...] = y


def _ff_chunks(d_ff, chunk):
    out, lo = [], 0
    while lo < d_ff:
        w = min(chunk, d_ff - lo)
        out.append((lo, w))
        lo += w
    return tuple(out)


def _ffn(x, layer, norm_g, w_gate, w_up, w_down, final_g, *, final_norm, tm=512, ff_chunk=512):
    n, d = x.shape
    d_ff = w_gate.shape[2]
    return pl.pallas_call(
        functools.partial(_ffn_kernel, chunks=_ff_chunks(d_ff, ff_chunk), final_norm=final_norm),
        grid=(n // tm,),
        in_specs=[pl.BlockSpec((tm, d), lambda i: (i, 0)),
                  _layer_spec(layer, 1, d),
                  _layer_spec(layer, d, d_ff),
                  _layer_spec(layer, d, d_ff),
                  _layer_spec(layer, d_ff, d),
                  pl.BlockSpec((1, d), lambda i: (0, 0), pipeline_mode=pl.Buffered(1))],
        out_specs=pl.BlockSpec((tm, d), lambda i: (i, 0)),
        out_shape=jax.ShapeDtypeStruct((n, d), F32),
        scratch_shapes=[pltpu.VMEM((tm, d_ff), BF16)],
        compiler_params=pltpu.CompilerParams(
            dimension_semantics=("arbitrary",), vmem_limit_bytes=VMEM_LIMIT_BYTES),
        name="ffn",
    )(x, norm_g, w_gate, w_up, w_down, final_g)


def _mixer_kernel(sink_ref, x_ref, cos_ref, sin_ref, g_ref, win_ref, cw_ref, cb_ref, lng_ref, lnb_ref,
                  wout_ref, o_ref, kv_ref, hist_ref, *, tm, layer):
    s_idx = pl.program_id(1)
    conv_c = cw_ref.shape[1]

    @pl.when(s_idx == 0)
    def _():
        kv_ref[:, 0:BLOCK, :] = jnp.zeros((8, BLOCK, LANES), BF16)
        hist_ref[...] = jnp.zeros((CONV_HIST, conv_c), F32)

    x = x_ref[...]
    h = _rms_norm(x, g_ref[...]).astype(BF16)
    p = jnp.dot(h, win_ref[...], preferred_element_type=F32)

    lane = lax.broadcasted_iota(jnp.int32, (1, LANES), 1)
    cos = cos_ref[...]
    sin = sin_ref[...]

    def rope(t):
        return t * cos + pltpu.roll(t, HEAD_DIM, 1) * sin

    q_end, k_end, v_end = ATTN_WIDTH, ATTN_WIDTH + KV_WIDTH, ATTN_WIDTH + 2 * KV_WIDTH
    q = [rope(p[:, c * LANES:(c + 1) * LANES]).astype(BF16) for c in range(ATTN_WIDTH // LANES)]
    k = rope(p[:, q_end:k_end])
    v = p[:, k_end:v_end]

    on_a = (lane & (HEAD_DIM - 1)) < HALF
    zero = jnp.zeros((), F32)
    rows = pl.ds(BLOCK, tm)
    kv_ref[0, rows, :] = jnp.where(on_a, k, zero).astype(BF16)
    kv_ref[1, rows, :] = jnp.where(on_a, zero, pltpu.roll(k, HALF, 1)).astype(BF16)
    kv_ref[2, rows, :] = jnp.where(on_a, pltpu.roll(k, LANES - HALF, 1), zero).astype(BF16)
    kv_ref[3, rows, :] = jnp.where(on_a, zero, k).astype(BF16)
    first = lane < HEAD_DIM
    v_sw = pltpu.roll(v, HEAD_DIM, 1)
    kv_ref[4, rows, :] = jnp.where(first, v, zero).astype(BF16)
    kv_ref[5, rows, :] = jnp.where(first, zero, v_sw).astype(BF16)
    kv_ref[6, rows, :] = jnp.where(first, v_sw, zero).astype(BF16)
    kv_ref[7, rows, :] = jnp.where(first, zero, v).astype(BF16)

    qi = lax.broadcasted_iota(jnp.int32, (BLOCK, 2 * BLOCK), 0)
    kj = lax.broadcasted_iota(jnp.int32, (BLOCK, 2 * BLOCK), 1)
    in_window = (kj > qi) & (kj <= qi + BLOCK)
    first_block_mask = in_window & ((kj >= BLOCK) | (s_idx > 0))
    neg = jnp.finfo(F32).min
    nt = (((1,), (1,)), ((), ()))

    def softmax_parts(s, mask, sink):
        s = jnp.where(mask, s, neg)
        m = jnp.maximum(jnp.max(s, axis=-1, keepdims=True), sink)
        e = jnp.exp(s - m)
        denom = jnp.sum(e, axis=-1, keepdims=True) + jnp.exp(sink - m)
        return e.astype(BF16), 1.0 / denom

    attn_cols = []
    for c in range(ATTN_WIDTH // LANES):
        j = c // (N_HEADS // N_KV_HEADS // 2)
        blocks = []
        for b in range(tm // BLOCK):
            mask = first_block_mask if b == 0 else in_window
            band = pl.ds(b * BLOCK, 2 * BLOCK)
            qb = q[c][b * BLOCK:(b + 1) * BLOCK]
            s_a = lax.dot_general(qb, kv_ref[2 * j, band, :], nt, preferred_element_type=F32)
            s_b = lax.dot_general(qb, kv_ref[2 * j + 1, band, :], nt, preferred_element_type=F32)
            p_a, r_a = softmax_parts(s_a, mask, sink_ref[layer, 2 * c])
            p_b, r_b = softmax_parts(s_b, mask, sink_ref[layer, 2 * c + 1])
            o_a = jnp.dot(p_a, kv_ref[4 + 2 * j, band, :], preferred_element_type=F32)
            o_b = jnp.dot(p_b, kv_ref[5 + 2 * j, band, :], preferred_element_type=F32)
            blocks.append(o_a * r_a + o_b * r_b)
        attn_cols.append(jnp.concatenate(blocks, axis=0))
    attn_out = jnp.concatenate(attn_cols, axis=1).astype(BF16)

    a = p[:, v_end:v_end + conv_c]
    gate = p[:, v_end + conv_c:v_end + 2 * conv_c]
    hg = a * _sigmoid(gate)
    hbuf = jnp.concatenate([hist_ref[...], hg], axis=0)
    hist_ref[...] = hg[tm - CONV_HIST:, :]
    n_rows = CONV_HIST + tm
    shifted = [hbuf] + [pltpu.roll(hbuf, n_rows - r, 0) for r in range(1, SUBLANES)]
    acc = jnp.broadcast_to(cb_ref[...], (tm, conv_c))
    for w in range(CONV_WIDTH):
        base, r = divmod(CONV_HIST - (CONV_WIDTH - 1) + w, SUBLANES)
        acc = acc + shifted[r][base * SUBLANES:base * SUBLANES + tm, :] * cw_ref[w:w + 1, :]
    mu = jnp.mean(acc, axis=-1, keepdims=True)
    xc = acc - mu
    y = xc * lax.rsqrt(jnp.mean(xc * xc, axis=-1, keepdims=True) + EPS) * lng_ref[...] + lnb_ref[...]
    conv_out = (y * _sigmoid(y)).astype(BF16)

    mixed = jnp.concatenate([attn_out, conv_out], axis=1)
    o_ref[...] = x + jnp.dot(mixed, wout_ref[...], preferred_element_type=F32)

    kv_ref[:, 0:BLOCK, :] = kv_ref[:, tm:tm + BLOCK, :]


def _mixer(x, layer, cos, sin, norm_g, w_in, conv_w, conv_b, ln_g, ln_b, sinks, w_out, *, tm=512):
    B, S, d = x.shape
    d_in = w_in.shape[2]
    conv_c = conv_w.shape[2]
    d_mix = w_out.shape[1]
    tile = lambda width: pl.BlockSpec((None, tm, width), lambda b, s: (b, s, 0))
    return pl.pallas_call(
        functools.partial(_mixer_kernel, tm=tm, layer=layer),
        grid=(B, S // tm),
        in_specs=[pl.BlockSpec(memory_space=pltpu.SMEM),
                  tile(d), tile(LANES), tile(LANES),
                  _layer_spec(layer, 1, d),
                  _layer_spec(layer, d, d_in),
                  _layer_spec(layer, CONV_WIDTH, conv_c),
                  _layer_spec(layer, 1, conv_c),
                  _layer_spec(layer, 1, conv_c),
                  _layer_spec(layer, 1, conv_c),
                  _layer_spec(layer, d_mix, d)],
        out_specs=tile(d),
        out_shape=jax.ShapeDtypeStruct((B, S, d), F32),
        scratch_shapes=[pltpu.VMEM((8, BLOCK + tm, LANES), BF16),
                        pltpu.VMEM((CONV_HIST, conv_c), F32)],
        compiler_params=pltpu.CompilerParams(
            dimension_semantics=("arbitrary", "arbitrary"), vmem_limit_bytes=VMEM_LIMIT_BYTES),
        name="mixer",
    )(sinks, x, cos, sin, norm_g, w_in, conv_w, conv_b, ln_g, ln_b, w_out)


def _prep_w_in(w_in):
    depth, d, d_in = w_in.shape
    n_rope = ATTN_WIDTH + KV_WIDTH
    qk = w_in[:, :, :n_rope].reshape(depth, d, n_rope // LANES, 2, 2, HALF)
    qk = qk.transpose(0, 1, 2, 4, 3, 5).reshape(depth, d, n_rope)
    scale = jnp.where(jnp.arange(n_rope) < ATTN_WIDTH, HEAD_DIM ** -0.5, 1.0).astype(F32)
    return jnp.concatenate([qk * scale, w_in[:, :, n_rope:]], axis=2).astype(BF16)


def kernel(x, positions, ffn1_norm, ffn1_w_gate, ffn1_w_up, ffn1_w_down, mix_norm, w_in, conv_w, conv_b,
           conv_ln_g, conv_ln_b, attn_sinks, w_out, ffn2_norm, ffn2_w_gate, ffn2_w_up, ffn2_w_down,
           final_norm):
    B, S, d = x.shape
    depth = w_in.shape[0]
    row = lambda p: p.reshape(depth, 1, p.shape[-1])
    cos, sin = _rope_tables(positions)
    ffn1 = (row(ffn1_norm), ffn1_w_gate.astype(BF16), ffn1_w_up.astype(BF16), ffn1_w_down.astype(BF16))
    ffn2 = (row(ffn2_norm), ffn2_w_gate.astype(BF16), ffn2_w_up.astype(BF16), ffn2_w_down.astype(BF16))
    mix = (row(mix_norm), _prep_w_in(w_in), conv_w, row(conv_b), row(conv_ln_g), row(conv_ln_b),
           attn_sinks, w_out.astype(BF16))
    final_g = final_norm.reshape(1, d)
    x = x.reshape(B * S, d)
    for l in range(depth):
        x = _ffn(x, l, *ffn1, final_g, final_norm=False)
        x = _mixer(x.reshape(B, S, d), l, cos, sin, *mix).reshape(B * S, d)
        x = _ffn(x, l, *ffn2, final_g, final_norm=(l == depth - 1))
    return x.reshape(B, S, d)
```

```python
import functools

import jax
import jax.numpy as jnp
from jax import lax
from jax.experimental import pallas as pl
from jax.experimental.pallas import tpu as pltpu

HEAD_DIM = 64
N_HEADS = 8
N_KV_HEADS = 2
ATTN_WIDTH = N_HEADS * HEAD_DIM
KV_WIDTH = N_KV_HEADS * HEAD_DIM
CONV_WIDTH = 31
BLOCK = 128
ROPE_THETA = 10000.0
EPS = 1e-5

LANES = 128
SUBLANES = 8
MXU_COLS = 256
VMEM_LIMIT_BYTES = 56 * 1024 * 1024

CONV_HIST = 32
HALF = HEAD_DIM // 2

BF16 = jnp.bfloat16
F32 = jnp.float32


def _sigmoid(x):
    return 1.0 / (1.0 + jnp.exp(-x))


def _rms_norm(x, g):
    return x * lax.rsqrt(jnp.mean(x * x, axis=-1, keepdims=True) + EPS) * g


def _layer_spec(layer, *shape):
    zeros = (0,) * len(shape)
    return pl.BlockSpec((None,) + shape, lambda *_: (layer,) + zeros, pipeline_mode=pl.Buffered(1))


def _rope_table_kernel(pos_ref, invf_ref, cos_ref, sin_ref):
    ang = pos_ref[...].astype(F32) * invf_ref[...]
    lane = lax.broadcasted_iota(jnp.int32, (1, LANES), 1)
    sign = jnp.where(lane < HEAD_DIM, -1.0, 1.0).astype(F32)
    for table, out_ref, scale in ((jnp.cos(ang), cos_ref, None), (jnp.sin(ang), sin_ref, sign)):
        for i in range(LANES // HALF):
            t = jnp.where((lane >= i * HALF) & (lane < (i + 1) * HALF), table, 0.0)
            t = t + pltpu.roll(t, 2 * HALF, 1)
            t = t + pltpu.roll(t, HALF, 1)
            out_ref[i] = t if scale is None else t * scale


def _rope_tables(positions):
    B, S = positions.shape
    n = B * S
    groups = LANES // HALF
    rows = n // groups
    inv_freq = 1.0 / (ROPE_THETA ** (jnp.arange(0, HEAD_DIM, 2, dtype=F32) / HEAD_DIM))
    invf = jnp.tile(inv_freq, groups).reshape(1, LANES)
    pos = jnp.repeat(positions.reshape(groups, rows).T, HALF, axis=1)
    tr = 512
    spec_out = pl.BlockSpec((groups, tr, LANES), lambda i: (0, i, 0))
    cos, sin = pl.pallas_call(
        _rope_table_kernel,
        grid=(rows // tr,),
        in_specs=[pl.BlockSpec((tr, LANES), lambda i: (i, 0)),
                  pl.BlockSpec((1, LANES), lambda i: (0, 0))],
        out_specs=[spec_out, spec_out],
        out_shape=[jax.ShapeDtypeStruct((groups, rows, LANES), F32)] * 2,
        name="rope_tables",
    )(pos, invf)
    return cos.reshape(B, S, LANES), sin.reshape(B, S, LANES)


def _ffn_kernel(x_ref, g_ref, wg_ref, wu_ref, wd_ref, fg_ref, o_ref, a_ref, *, chunks, final_norm):
    x = x_ref[...]
    h = _rms_norm(x, g_ref[...]).astype(BF16)
    for lo, width in chunks:
        gate = jnp.dot(h, wg_ref[:, lo:lo + width], preferred_element_type=F32)
        up = jnp.dot(h, wu_ref[:, lo:lo + width], preferred_element_type=F32)
        a_ref[:, lo:lo + width] = (gate * _sigmoid(gate) * up).astype(BF16)
    y = x + 0.5 * jnp.dot(a_ref[...], wd_ref[...], preferred_element_type=F32)
    if final_norm:
        y = _rms_norm(y, fg_ref[...])
    o_ref[...] = y


def _ff_chunks(d_ff, chunk):
    out, lo = [], 0
    while lo < d_ff:
        w = min(chunk, d_ff - lo)
        out.append((lo, w))
        lo += w
    return tuple(out)


def _ffn(x, layer, norm_g, w_gate, w_up, w_down, final_g, *, final_norm, tm=1024, ff_chunk=512):
    n, d = x.shape
    d_ff = w_gate.shape[2]
    return pl.pallas_call(
        functools.partial(_ffn_kernel, chunks=_ff_chunks(d_ff, ff_chunk), final_norm=final_norm),
        grid=(n // tm,),
        in_specs=[pl.BlockSpec((tm, d), lambda i: (i, 0)),
                  _layer_spec(layer, 1, d),
                  _layer_spec(layer, d, d_ff),
                  _layer_spec(layer, d, d_ff),
                  _layer_spec(layer, d_ff, d),
                  pl.BlockSpec((1, d), lambda i: (0, 0), pipeline_mode=pl.Buffered(1))],
        out_specs=pl.BlockSpec((tm, d), lambda i: (i, 0)),
        out_shape=jax.ShapeDtypeStruct((n, d), F32),
        scratch_shapes=[pltpu.VMEM((tm, d_ff), BF16)],
        compiler_params=pltpu.CompilerParams(
            dimension_semantics=("arbitrary",), vmem_limit_bytes=VMEM_LIMIT_BYTES),
        name="ffn",
    )(x, norm_g, w_gate, w_up, w_down, final_g)


def _mixer_kernel(sink_ref, x_ref, cos_ref, sin_ref, g_ref, win_ref, cw_ref, cb_ref, lng_ref, lnb_ref,
                  wout_ref, o_ref, kv_ref, hg_ref, sh_ref, *, tm, ts, layer):
    s_idx = pl.program_id(1)
    conv_c = cw_ref.shape[1]
    n_sub = tm // ts
    n_blocks = ts // BLOCK
    q_end, k_end, v_end = ATTN_WIDTH, ATTN_WIDTH + KV_WIDTH, ATTN_WIDTH + 2 * KV_WIDTH

    @pl.when(s_idx == 0)
    def _():
        kv_ref[:, 0:BLOCK, :] = jnp.zeros((8, BLOCK, LANES), BF16)
        hg_ref[0:CONV_HIST, :] = jnp.zeros((CONV_HIST, conv_c), F32)

    lane = lax.broadcasted_iota(jnp.int32, (1, LANES), 1)
    on_a = (lane & (HEAD_DIM - 1)) < HALF
    first = lane < HEAD_DIM
    zero = jnp.zeros((), F32)

    qi = lax.broadcasted_iota(jnp.int32, (BLOCK, 2 * BLOCK), 0)
    kj = lax.broadcasted_iota(jnp.int32, (BLOCK, 2 * BLOCK), 1)
    in_window = (kj > qi) & (kj <= qi + BLOCK)
    first_block_mask = in_window & ((kj >= BLOCK) | (s_idx > 0))
    neg = jnp.finfo(F32).min
    nt = (((1,), (1,)), ((), ()))

    def project(sub, out):
        lo = sub * ts
        tile_rows = pl.ds(lo, ts)
        x = x_ref[tile_rows, :]
        out["x"] = x
        h = _rms_norm(x, g_ref[...]).astype(BF16)
        cos = cos_ref[tile_rows, :]
        sin = sin_ref[tile_rows, :]

        def proj(lo_col, width):
            return jnp.dot(h, win_ref[:, lo_col:lo_col + width], preferred_element_type=F32)

        def rope(t):
            return t * cos + pltpu.roll(t, HEAD_DIM, 1) * sin

        out["q"] = []
        for lo_col in range(0, q_end, MXU_COLS):
            pq = proj(lo_col, MXU_COLS)
            out["q"] += [rope(pq[:, i * LANES:(i + 1) * LANES]).astype(BF16) for i in range(MXU_COLS // LANES)]
            yield

        pkv = proj(q_end, 2 * KV_WIDTH)
        k = rope(pkv[:, :KV_WIDTH])
        v = pkv[:, KV_WIDTH:]
        rows = pl.ds(BLOCK + lo, ts)
        kv_ref[0, rows, :] = jnp.where(on_a, k, zero).astype(BF16)
        kv_ref[1, rows, :] = jnp.where(on_a, zero, pltpu.roll(k, HALF, 1)).astype(BF16)
        kv_ref[2, rows, :] = jnp.where(on_a, pltpu.roll(k, LANES - HALF, 1), zero).astype(BF16)
        kv_ref[3, rows, :] = jnp.where(on_a, zero, k).astype(BF16)
        v_sw = pltpu.roll(v, HEAD_DIM, 1)
        kv_ref[4, rows, :] = jnp.where(first, v, zero).astype(BF16)
        kv_ref[5, rows, :] = jnp.where(first, zero, v_sw).astype(BF16)
        kv_ref[6, rows, :] = jnp.where(first, v_sw, zero).astype(BF16)
        kv_ref[7, rows, :] = jnp.where(first, zero, v).astype(BF16)
        yield

        n_rows = CONV_HIST + ts
        for lo_col in range(0, conv_c, MXU_COLS):
            a = proj(v_end + lo_col, MXU_COLS)
            yield
            gate = proj(v_end + conv_c + lo_col, MXU_COLS)
            cols = pl.ds(lo_col, MXU_COLS)
            hg_ref[pl.ds(CONV_HIST + lo, ts), cols] = a * _sigmoid(gate)
            hbuf = hg_ref[pl.ds(lo, n_rows), cols]
            for r in range(1, SUBLANES):
                sh_ref[sub, r - 1, pl.ds(SUBLANES - r, n_rows), cols] = hbuf
            yield

    def conv_piece(sub, b, c):
        lo = sub * ts
        cols = pl.ds(c * LANES, LANES)
        acc = jnp.broadcast_to(cb_ref[:, cols], (BLOCK, LANES))
        for w in range(CONV_WIDTH):
            base, r = divmod(CONV_HIST - (CONV_WIDTH - 1) + w, SUBLANES)
            if r == 0:
                slab = hg_ref[pl.ds(lo + base * SUBLANES + b * BLOCK, BLOCK), cols]
            else:
                slab = sh_ref[sub, r - 1, pl.ds((base + 1) * SUBLANES + b * BLOCK, BLOCK), cols]
            acc = acc + slab * cw_ref[w:w + 1, cols]
        return acc

    def mix(sub, proj_out):
        lo = sub * ts
        q = proj_out["q"]
        n_cols = ATTN_WIDTH // LANES
        units = [(c, b) for c in range(n_cols) for b in range(n_blocks)]
        conv = {}

        def conv_some(first_unit, count):
            for c, b in units[first_unit:first_unit + count]:
                conv[(c, b)] = conv_piece(sub, b, c)

        per_phase = -(-len(units) // 4)
        heads = [(c, b, e) for c, b in units for e in range(2)]
        kv_of = lambda c: c // (N_HEADS // N_KV_HEADS // 2)
        band_of = lambda b: pl.ds(lo + b * BLOCK, 2 * BLOCK)

        scores = {}
        for c, b, e in heads:
            qb = q[c][b * BLOCK:(b + 1) * BLOCK]
            scores[(c, b, e)] = lax.dot_general(qb, kv_ref[2 * kv_of(c) + e, band_of(b), :], nt,
                                                preferred_element_type=F32)
        conv_some(0, per_phase)
        yield

        row_max = {}
        for c, b, e in heads:
            mask = first_block_mask if lo + b == 0 else in_window
            scores[(c, b, e)] = jnp.where(mask, scores[(c, b, e)], neg)
            row_max[(c, b, e)] = jnp.maximum(jnp.max(scores[(c, b, e)], axis=-1, keepdims=True),
                                             sink_ref[layer, 2 * c + e])
        conv_some(per_phase, per_phase)
        yield

        probs, recip = {}, {}
        for c, b, e in heads:
            m = row_max[(c, b, e)]
            ex = jnp.exp(scores[(c, b, e)] - m)
            denom = jnp.sum(ex, axis=-1, keepdims=True) + jnp.exp(sink_ref[layer, 2 * c + e] - m)
            probs[(c, b, e)] = ex.astype(BF16)
            recip[(c, b, e)] = 1.0 / denom
        conv_some(2 * per_phase, per_phase)
        yield

        outs = {}
        for c, b in units:
            o = [jnp.dot(probs[(c, b, e)], kv_ref[4 + 2 * kv_of(c) + e, band_of(b), :],
                         preferred_element_type=F32) * recip[(c, b, e)] for e in range(2)]
            outs[(c, b)] = o[0] + o[1]
        conv_some(3 * per_phase, per_phase)
        yield

        gather = lambda d: jnp.concatenate(
            [jnp.concatenate([d[(c, b)] for b in range(n_blocks)], axis=0) for c in range(n_cols)], axis=1)
        attn_out = gather(outs).astype(BF16)
        acc = gather(conv)
        mu = jnp.mean(acc, axis=-1, keepdims=True)
        xc = acc - mu
        y = xc * lax.rsqrt(jnp.mean(xc * xc, axis=-1, keepdims=True) + EPS) * lng_ref[...] + lnb_ref[...]
        conv_out = (y * _sigmoid(y)).astype(BF16)
        mixed = jnp.concatenate([attn_out, conv_out], axis=1)
        o_ref[pl.ds(lo, ts), :] = proj_out["x"] + jnp.dot(mixed, wout_ref[...], preferred_element_type=F32)

    proj_out = {}
    for _ in project(0, proj_out):
        pass
    for sub in range(n_sub):
        next_out = {}
        stages = [mix(sub, proj_out)]
        if sub + 1 < n_sub:
            stages.append(project(sub + 1, next_out))
        while stages:
            for g in list(stages):
                if next(g, StopIteration) is StopIteration:
                    stages.remove(g)
        proj_out = next_out

    kv_ref[:, 0:BLOCK, :] = kv_ref[:, tm:tm + BLOCK, :]
    hg_ref[0:CONV_HIST, :] = hg_ref[tm:tm + CONV_HIST, :]


def _mixer(x, layer, cos, sin, norm_g, w_in, conv_w, conv_b, ln_g, ln_b, sinks, w_out, *, tm=512, ts=256):
    B, S, d = x.shape
    d_in = w_in.shape[2]
    conv_c = conv_w.shape[2]
    d_mix = w_out.shape[1]
    assert conv_c == ATTN_WIDTH
    tile = lambda width: pl.BlockSpec((None, tm, width), lambda b, s: (b, s, 0))
    return pl.pallas_call(
        functools.partial(_mixer_kernel, tm=tm, ts=ts, layer=layer),
        grid=(B, S // tm),
        in_specs=[pl.BlockSpec(memory_space=pltpu.SMEM),
                  tile(d), tile(LANES), tile(LANES),
                  _layer_spec(layer, 1, d),
                  _layer_spec(layer, d, d_in),
                  _layer_spec(layer, CONV_WIDTH, conv_c),
                  _layer_spec(layer, 1, conv_c),
                  _layer_spec(layer, 1, conv_c),
                  _layer_spec(layer, 1, conv_c),
                  _layer_spec(layer, d_mix, d)],
        out_specs=tile(d),
        out_shape=jax.ShapeDtypeStruct((B, S, d), F32),
        scratch_shapes=[pltpu.VMEM((8, BLOCK + tm, LANES), BF16),
                        pltpu.VMEM((CONV_HIST + tm, conv_c), F32),
                        pltpu.VMEM((tm // ts, SUBLANES - 1, SUBLANES + CONV_HIST + ts, conv_c), F32)],
        compiler_params=pltpu.CompilerParams(
            dimension_semantics=("arbitrary", "arbitrary"), vmem_limit_bytes=VMEM_LIMIT_BYTES),
        name="mixer",
    )(sinks, x, cos, sin, norm_g, w_in, conv_w, conv_b, ln_g, ln_b, w_out)


def _prep_w_in(w_in):
    depth, d, d_in = w_in.shape
    n_rope = ATTN_WIDTH + KV_WIDTH
    qk = w_in[:, :, :n_rope].reshape(depth, d, n_rope // LANES, 2, 2, HALF)
    qk = qk.transpose(0, 1, 2, 4, 3, 5).reshape(depth, d, n_rope)
    scale = jnp.where(jnp.arange(n_rope) < ATTN_WIDTH, HEAD_DIM ** -0.5, 1.0).astype(F32)
    return jnp.concatenate([qk * scale, w_in[:, :, n_rope:]], axis=2).astype(BF16)


def kernel(x, positions, ffn1_norm, ffn1_w_gate, ffn1_w_up, ffn1_w_down, mix_norm, w_in, conv_w, conv_b,
           conv_ln_g, conv_ln_b, attn_sinks, w_out, ffn2_norm, ffn2_w_gate, ffn2_w_up, ffn2_w_down,
           final_norm):
    B, S, d = x.shape
    depth = w_in.shape[0]
    row = lambda p: p.reshape(depth, 1, p.shape[-1])
    cos, sin = _rope_tables(positions)
    ffn1 = (row(ffn1_norm), ffn1_w_gate.astype(BF16), ffn1_w_up.astype(BF16), ffn1_w_down.astype(BF16))
    ffn2 = (row(ffn2_norm), ffn2_w_gate.astype(BF16), ffn2_w_up.astype(BF16), ffn2_w_down.astype(BF16))
    mix = (row(mix_norm), _prep_w_in(w_in), conv_w, row(conv_b), row(conv_ln_g), row(conv_ln_b),
           attn_sinks, w_out.astype(BF16))
    final_g = final_norm.reshape(1, d)
    x = x.reshape(B * S, d)
    for l in range(depth):
        x = _ffn(x, l, *ffn1, final_g, final_norm=False)
        x = _mixer(x.reshape(B, S, d), l, cos, sin, *mix).reshape(B * S, d)
        x = _ffn(x, l, *ffn2, final_g, final_norm=(l == depth - 1))
    return x.reshape(B, S, d)
```

```python
import functools

import jax
import jax.numpy as jnp
from jax import lax
from jax.experimental import pallas as pl
from jax.experimental.pallas import tpu as pltpu

HEAD_DIM = 64
N_HEADS = 8
N_KV_HEADS = 2
ATTN_WIDTH = N_HEADS * HEAD_DIM
KV_WIDTH = N_KV_HEADS * HEAD_DIM
CONV_WIDTH = 31
BLOCK = 128
ROPE_THETA = 10000.0
EPS = 1e-5

LANES = 128
SUBLANES = 8
MXU_COLS = 256
VMEM_LIMIT_BYTES = 56 * 1024 * 1024

CONV_HIST = 32
HALF = HEAD_DIM // 2

BF16 = jnp.bfloat16
F32 = jnp.float32


def _sigmoid(x):
    return 1.0 / (1.0 + jnp.exp(-x))


def _rms_norm(x, g):
    return x * lax.rsqrt(jnp.mean(x * x, axis=-1, keepdims=True) + EPS) * g


def _layer_spec(layer, *shape):
    zeros = (0,) * len(shape)
    return pl.BlockSpec((None,) + shape, lambda *_: (layer,) + zeros, pipeline_mode=pl.Buffered(1))


def _rope_table_kernel(pos_ref, invf_ref, cos_ref, sin_ref):
    ang = pos_ref[...].astype(F32) * invf_ref[...]
    lane = lax.broadcasted_iota(jnp.int32, (1, LANES), 1)
    sign = jnp.where((lane & (HEAD_DIM - 1)) < HALF, -1.0, 1.0).astype(F32)
    for table, out_ref, scale in ((jnp.cos(ang), cos_ref, None), (jnp.sin(ang), sin_ref, sign)):
        for i in range(LANES // HALF):
            t = jnp.where((lane >= i * HALF) & (lane < (i + 1) * HALF), table, 0.0)
            t = t + pltpu.roll(t, 2 * HALF, 1)
            t = t + pltpu.roll(t, HALF, 1)
            out_ref[i] = t if scale is None else t * scale


def _rope_tables(positions):
    B, S = positions.shape
    n = B * S
    groups = LANES // HALF
    rows = n // groups
    inv_freq = 1.0 / (ROPE_THETA ** (jnp.arange(0, HEAD_DIM, 2, dtype=F32) / HEAD_DIM))
    invf = jnp.tile(inv_freq, groups).reshape(1, LANES)
    pos = jnp.repeat(positions.reshape(groups, rows).T, HALF, axis=1)
    tr = 512
    spec_out = pl.BlockSpec((groups, tr, LANES), lambda i: (0, i, 0))
    cos, sin = pl.pallas_call(
        _rope_table_kernel,
        grid=(rows // tr,),
        in_specs=[pl.BlockSpec((tr, LANES), lambda i: (i, 0)),
                  pl.BlockSpec((1, LANES), lambda i: (0, 0))],
        out_specs=[spec_out, spec_out],
        out_shape=[jax.ShapeDtypeStruct((groups, rows, LANES), F32)] * 2,
        name="rope_tables",
    )(pos, invf)
    return cos.reshape(B, S, LANES), sin.reshape(B, S, LANES)


def _ffn_kernel(x_ref, g_ref, wg_ref, wu_ref, wd_ref, fg_ref, o_ref, a_ref, *, chunks, final_norm, parts):
    tm = x_ref.shape[0]
    rows = tm // parts

    def hidden(h, lo, width):
        gate = jnp.dot(h, wg_ref[:, lo:lo + width], preferred_element_type=F32)
        up = jnp.dot(h, wu_ref[:, lo:lo + width], preferred_element_type=F32)
        return (gate * _sigmoid(gate) * up).astype(BF16)

    h_parts = []
    lo, width = chunks[0]
    for j in range(parts):
        part = pl.ds(j * rows, rows)
        h_parts.append(_rms_norm(x_ref[part, :], g_ref[...]).astype(BF16))
        a_ref[part, lo:lo + width] = hidden(h_parts[j], lo, width)
    h = jnp.concatenate(h_parts, axis=0)
    for lo, width in chunks[1:]:
        a_ref[:, lo:lo + width] = hidden(h, lo, width)
    y = x_ref[...] + 0.5 * jnp.dot(a_ref[...], wd_ref[...], preferred_element_type=F32)
    if final_norm:
        y = _rms_norm(y, fg_ref[...])
    o_ref[...] = y


def _ff_chunks(d_ff, chunk):
    out, lo = [], 0
    while lo < d_ff:
        w = min(chunk, d_ff - lo)
        out.append((lo, w))
        lo += w
    return tuple(out)


def _ffn(x, layer, norm_g, w_gate, w_up, w_down, final_g, *, final_norm, tm=1024, ff_chunk=512, parts=4):
    n, d = x.shape
    d_ff = w_gate.shape[2]
    return pl.pallas_call(
        functools.partial(_ffn_kernel, chunks=_ff_chunks(d_ff, ff_chunk), final_norm=final_norm, parts=parts),
        grid=(n // tm,),
        in_specs=[pl.BlockSpec((tm, d), lambda i: (i, 0)),
                  _layer_spec(layer, 1, d),
                  _layer_spec(layer, d, d_ff),
                  _layer_spec(layer, d, d_ff),
                  _layer_spec(layer, d_ff, d),
                  pl.BlockSpec((1, d), lambda i: (0, 0), pipeline_mode=pl.Buffered(1))],
        out_specs=pl.BlockSpec((tm, d), lambda i: (i, 0)),
        out_shape=jax.ShapeDtypeStruct((n, d), F32),
        scratch_shapes=[pltpu.VMEM((tm, d_ff), BF16)],
        compiler_params=pltpu.CompilerParams(
            dimension_semantics=("arbitrary",), vmem_limit_bytes=VMEM_LIMIT_BYTES),
        name="ffn",
    )(x, norm_g, w_gate, w_up, w_down, final_g)


def _mixer_kernel(sink_ref, x_ref, cos_ref, sin_ref, g_ref, win_ref, cw_ref, cb_ref, lng_ref, lnb_ref,
                  wout_ref, o_ref, kv_ref, hg_ref, sh_ref, *, tm, ts, layer):
    s_idx = pl.program_id(1)
    conv_c = cw_ref.shape[1]
    n_sub = tm // ts
    n_blocks = ts // BLOCK
    q_end, k_end, v_end = ATTN_WIDTH, ATTN_WIDTH + KV_WIDTH, ATTN_WIDTH + 2 * KV_WIDTH

    @pl.when(s_idx == 0)
    def _():
        kv_ref[:, 0:BLOCK, :] = jnp.zeros((8, BLOCK, LANES), BF16)
        hg_ref[0:CONV_HIST, :] = jnp.zeros((CONV_HIST, conv_c), F32)

    lane = lax.broadcasted_iota(jnp.int32, (1, LANES), 1)
    low_half = (lane & (HEAD_DIM - 1)) < HALF
    first = lane < HEAD_DIM
    zero = jnp.zeros((), F32)

    qi = lax.broadcasted_iota(jnp.int32, (BLOCK, 2 * BLOCK), 0)
    kj = lax.broadcasted_iota(jnp.int32, (BLOCK, 2 * BLOCK), 1)
    in_window = (kj > qi) & (kj <= qi + BLOCK)
    first_block_mask = in_window & ((kj >= BLOCK) | (s_idx > 0))
    neg = jnp.finfo(F32).min
    nt = (((1,), (1,)), ((), ()))

    def project(sub, out):
        lo = sub * ts
        tile_rows = pl.ds(lo, ts)
        x = x_ref[tile_rows, :]
        out["x"] = x
        h = _rms_norm(x, g_ref[...]).astype(BF16)
        cos = cos_ref[tile_rows, :]
        sin = sin_ref[tile_rows, :]

        def proj(lo_col, width):
            return jnp.dot(h, win_ref[:, lo_col:lo_col + width], preferred_element_type=F32)

        def rope(t):
            partner = jnp.where(low_half, pltpu.roll(t, LANES - HALF, 1), pltpu.roll(t, HALF, 1))
            return t * cos + partner * sin

        out["q"] = []
        for lo_col in range(0, q_end, MXU_COLS):
            pq = proj(lo_col, MXU_COLS)
            out["q"] += [rope(pq[:, i * LANES:(i + 1) * LANES]).astype(BF16) for i in range(MXU_COLS // LANES)]
            yield

        pkv = proj(q_end, 2 * KV_WIDTH)
        k = rope(pkv[:, :KV_WIDTH])
        v = pkv[:, KV_WIDTH:]
        rows = pl.ds(BLOCK + lo, ts)
        k_sw = pltpu.roll(k, HEAD_DIM, 1)
        v_sw = pltpu.roll(v, HEAD_DIM, 1)
        for i, t in enumerate((jnp.where(first, k, zero), jnp.where(first, zero, k_sw),
                               jnp.where(first, k_sw, zero), jnp.where(first, zero, k),
                               jnp.where(first, v, zero), jnp.where(first, zero, v_sw),
                               jnp.where(first, v_sw, zero), jnp.where(first, zero, v))):
            kv_ref[i, rows, :] = t.astype(BF16)
        yield

        n_rows = CONV_HIST + ts
        for lo_col in range(0, conv_c, MXU_COLS):
            a = proj(v_end + lo_col, MXU_COLS)
            yield
            gate = proj(v_end + conv_c + lo_col, MXU_COLS)
            cols = pl.ds(lo_col, MXU_COLS)
            hg_ref[pl.ds(CONV_HIST + lo, ts), cols] = a * _sigmoid(gate)
            hbuf = hg_ref[pl.ds(lo, n_rows), cols]
            for r in range(1, SUBLANES):
                sh_ref[sub % 2, r - 1, pl.ds(SUBLANES - r, n_rows), cols] = hbuf
            yield

    def conv_piece(sub, b, c):
        lo = sub * ts
        cols = pl.ds(c * LANES, LANES)
        acc = jnp.broadcast_to(cb_ref[:, cols], (BLOCK, LANES))
        for w in range(CONV_WIDTH):
            base, r = divmod(CONV_HIST - (CONV_WIDTH - 1) + w, SUBLANES)
            if r == 0:
                slab = hg_ref[pl.ds(lo + base * SUBLANES + b * BLOCK, BLOCK), cols]
            else:
                slab = sh_ref[sub % 2, r - 1, pl.ds((base + 1) * SUBLANES + b * BLOCK, BLOCK), cols]
            acc = acc + slab * cw_ref[w:w + 1, cols]
        return acc

    def mix(sub, proj_out):
        lo = sub * ts
        q = proj_out["q"]
        n_cols = ATTN_WIDTH // LANES
        units = [(c, b) for c in range(n_cols) for b in range(n_blocks)]
        conv = {}

        def conv_some(first_unit, count):
            for c, b in units[first_unit:first_unit + count]:
                conv[(c, b)] = conv_piece(sub, b, c)

        per_phase = -(-len(units) // 4)
        heads = [(c, b, e) for c, b in units for e in range(2)]
        kv_of = lambda c: c // (N_HEADS // N_KV_HEADS // 2)
        band_of = lambda b: pl.ds(lo + b * BLOCK, 2 * BLOCK)

        scores = {}
        for c, b, e in heads:
            qb = q[c][b * BLOCK:(b + 1) * BLOCK]
            scores[(c, b, e)] = lax.dot_general(qb, kv_ref[2 * kv_of(c) + e, band_of(b), :], nt,
                                                preferred_element_type=F32)
        conv_some(0, per_phase)
        yield

        row_max = {}
        for c, b, e in heads:
            mask = first_block_mask if lo + b == 0 else in_window
            scores[(c, b, e)] = jnp.where(mask, scores[(c, b, e)], neg)
            row_max[(c, b, e)] = jnp.maximum(jnp.max(scores[(c, b, e)], axis=-1, keepdims=True),
                                             sink_ref[layer, 2 * c + e])
        conv_some(per_phase, per_phase)
        yield

        probs, recip = {}, {}
        for c, b, e in heads:
            m = row_max[(c, b, e)]
            ex = jnp.exp(scores[(c, b, e)] - m)
            denom = jnp.sum(ex, axis=-1, keepdims=True) + jnp.exp(sink_ref[layer, 2 * c + e] - m)
            probs[(c, b, e)] = ex.astype(BF16)
            recip[(c, b, e)] = 1.0 / denom
        conv_some(2 * per_phase, per_phase)
        yield

        outs = {}
        for c, b in units:
            o = [jnp.dot(probs[(c, b, e)], kv_ref[4 + 2 * kv_of(c) + e, band_of(b), :],
                         preferred_element_type=F32) * recip[(c, b, e)] for e in range(2)]
            outs[(c, b)] = o[0] + o[1]
        conv_some(3 * per_phase, per_phase)
        yield

        gather = lambda d: jnp.concatenate(
            [jnp.concatenate([d[(c, b)] for b in range(n_blocks)], axis=0) for c in range(n_cols)], axis=1)
        attn_out = gather(outs).astype(BF16)
        acc = gather(conv)
        mu = jnp.mean(acc, axis=-1, keepdims=True)
        xc = acc - mu
        y = xc * lax.rsqrt(jnp.mean(xc * xc, axis=-1, keepdims=True) + EPS) * lng_ref[...] + lnb_ref[...]
        conv_out = (y * _sigmoid(y)).astype(BF16)
        mixed = jnp.concatenate([attn_out, conv_out], axis=1)
        o_ref[pl.ds(lo, ts), :] = proj_out["x"] + jnp.dot(mixed, wout_ref[...], preferred_element_type=F32)

    proj_out = {}
    for _ in project(0, proj_out):
        pass
    for sub in range(n_sub):
        next_out = {}
        stages = [mix(sub, proj_out)]
        if sub + 1 < n_sub:
            stages.append(project(sub + 1, next_out))
        while stages:
            for g in list(stages):
                if next(g, StopIteration) is StopIteration:
                    stages.remove(g)
        proj_out = next_out

    kv_ref[:, 0:BLOCK, :] = kv_ref[:, tm:tm + BLOCK, :]
    hg_ref[0:CONV_HIST, :] = hg_ref[tm:tm + CONV_HIST, :]


def _mixer(x, layer, cos, sin, norm_g, w_in, conv_w, conv_b, ln_g, ln_b, sinks, w_out, *, tm=1024, ts=256):
    B, S, d = x.shape
    d_in = w_in.shape[2]
    conv_c = conv_w.shape[2]
    d_mix = w_out.shape[1]
    assert conv_c == ATTN_WIDTH
    tile = lambda width: pl.BlockSpec((None, tm, width), lambda b, s: (b, s, 0))
    return pl.pallas_call(
        functools.partial(_mixer_kernel, tm=tm, ts=ts, layer=layer),
        grid=(B, S // tm),
        in_specs=[pl.BlockSpec(memory_space=pltpu.SMEM),
                  tile(d), tile(LANES), tile(LANES),
                  _layer_spec(layer, 1, d),
                  _layer_spec(layer, d, d_in),
                  _layer_spec(layer, CONV_WIDTH, conv_c),
                  _layer_spec(layer, 1, conv_c),
                  _layer_spec(layer, 1, conv_c),
                  _layer_spec(layer, 1, conv_c),
                  _layer_spec(layer, d_mix, d)],
        out_specs=tile(d),
        out_shape=jax.ShapeDtypeStruct((B, S, d), F32),
        scratch_shapes=[pltpu.VMEM((8, BLOCK + tm, LANES), BF16),
                        pltpu.VMEM((CONV_HIST + tm, conv_c), F32),
                        pltpu.VMEM((2, SUBLANES - 1, SUBLANES + CONV_HIST + ts, conv_c), F32)],
        compiler_params=pltpu.CompilerParams(
            dimension_semantics=("arbitrary", "arbitrary"), vmem_limit_bytes=VMEM_LIMIT_BYTES),
        name="mixer",
    )(sinks, x, cos, sin, norm_g, w_in, conv_w, conv_b, ln_g, ln_b, w_out)


def _prep_w_in(w_in):
    scale = jnp.where(jnp.arange(w_in.shape[2]) < ATTN_WIDTH, HEAD_DIM ** -0.5, 1.0).astype(F32)
    return (w_in * scale).astype(BF16)


def kernel(x, positions, ffn1_norm, ffn1_w_gate, ffn1_w_up, ffn1_w_down, mix_norm, w_in, conv_w, conv_b,
           conv_ln_g, conv_ln_b, attn_sinks, w_out, ffn2_norm, ffn2_w_gate, ffn2_w_up, ffn2_w_down,
           final_norm):
    B, S, d = x.shape
    depth = w_in.shape[0]
    row = lambda p: p.reshape(depth, 1, p.shape[-1])
    cos, sin = _rope_tables(positions)
    ffn1 = (row(ffn1_norm), ffn1_w_gate.astype(BF16), ffn1_w_up.astype(BF16), ffn1_w_down.astype(BF16))
    ffn2 = (row(ffn2_norm), ffn2_w_gate.astype(BF16), ffn2_w_up.astype(BF16), ffn2_w_down.astype(BF16))
    mix = (row(mix_norm), _prep_w_in(w_in), conv_w, row(conv_b), row(conv_ln_g), row(conv_ln_b),
           attn_sinks, w_out.astype(BF16))
    final_g = final_norm.reshape(1, d)
    x = x.reshape(B * S, d)
    for l in range(depth):
        x = _ffn(x, l, *ffn1, final_g, final_norm=False)
        x = _mixer(x.reshape(B, S, d), l, cos, sin, *mix).reshape(B * S, d)
        x = _ffn(x, l, *ffn2, final_g, final_norm=(l == depth - 1))
    return x.reshape(B, S, d)
```

```python
import functools

import jax
import jax.numpy as jnp
from jax import lax
from jax.experimental import pallas as pl
from jax.experimental.pallas import tpu as pltpu

HEAD_DIM = 64
N_HEADS = 8
N_KV_HEADS = 2
ATTN_WIDTH = N_HEADS * HEAD_DIM
KV_WIDTH = N_KV_HEADS * HEAD_DIM
CONV_WIDTH = 31
BLOCK = 128
ROPE_THETA = 10000.0
EPS = 1e-5

LANES = 128
SUBLANES = 8
MXU_COLS = 256
VMEM_LIMIT_BYTES = 56 * 1024 * 1024

CONV_HIST = 32
FFN_STAGE_SLOTS = 3
HALF = HEAD_DIM // 2

BF16 = jnp.bfloat16
F32 = jnp.float32


def _sigmoid(x):
    return 1.0 / (1.0 + jnp.exp(-x))


def _rms_norm(x, g):
    return x * lax.rsqrt(jnp.mean(x * x, axis=-1, keepdims=True) + EPS) * g


def _layer_spec(layer, *shape):
    zeros = (0,) * len(shape)
    return pl.BlockSpec((None,) + shape, lambda *_: (layer,) + zeros, pipeline_mode=pl.Buffered(1))


def _rope_table_kernel(pos_ref, invf_ref, cos_ref, sin_ref):
    ang = pos_ref[...].astype(F32) * invf_ref[...]
    lane = lax.broadcasted_iota(jnp.int32, (1, LANES), 1)
    sign = jnp.where((lane & (HEAD_DIM - 1)) < HALF, -1.0, 1.0).astype(F32)
    for table, out_ref, scale in ((jnp.cos(ang), cos_ref, None), (jnp.sin(ang), sin_ref, sign)):
        for i in range(LANES // HALF):
            t = jnp.where((lane >= i * HALF) & (lane < (i + 1) * HALF), table, 0.0)
            t = t + pltpu.roll(t, 2 * HALF, 1)
            t = t + pltpu.roll(t, HALF, 1)
            out_ref[i] = t if scale is None else t * scale


def _rope_tables(positions):
    B, S = positions.shape
    n = B * S
    groups = LANES // HALF
    rows = n // groups
    inv_freq = 1.0 / (ROPE_THETA ** (jnp.arange(0, HEAD_DIM, 2, dtype=F32) / HEAD_DIM))
    invf = jnp.tile(inv_freq, groups).reshape(1, LANES)
    pos = jnp.repeat(positions.reshape(groups, rows).T, HALF, axis=1)
    tr = 512
    spec_out = pl.BlockSpec((groups, tr, LANES), lambda i: (0, i, 0))
    cos, sin = pl.pallas_call(
        _rope_table_kernel,
        grid=(rows // tr,),
        in_specs=[pl.BlockSpec((tr, LANES), lambda i: (i, 0)),
                  pl.BlockSpec((1, LANES), lambda i: (0, 0))],
        out_specs=[spec_out, spec_out],
        out_shape=[jax.ShapeDtypeStruct((groups, rows, LANES), F32)] * 2,
        name="rope_tables",
    )(pos, invf)
    return cos.reshape(B, S, LANES), sin.reshape(B, S, LANES)


def _ffn_kernel(x_ref, g_ref, wg_hbm, wu_hbm, wd_hbm, fg_ref, o_ref, a_ref, wg_ref, wu_ref, wd_ref, stage_ref,
                sem, *, layer, chunks, final_norm, parts):
    tm, d = x_ref.shape
    d_ff = wg_ref.shape[1]
    rows = tm // parts
    n_slots, stage_rows, stage_cols = stage_ref.shape

    down_pieces = [(r0, min(stage_rows, d_ff - r0), c0, stage_cols)
                   for r0 in range(0, d_ff, stage_rows) for c0 in range(0, d, stage_cols)]
    n_down, n_chunks = len(down_pieces), len(chunks)
    pieces, after_chunk = [], []
    for c, (lo, width) in enumerate(chunks):
        pieces += [(wg_hbm, wg_ref, 0, d, lo, width), (wu_hbm, wu_ref, 0, d, lo, width)]
        mine = down_pieces[c * n_down // n_chunks:(c + 1) * n_down // n_chunks]
        pieces += [(wd_hbm, wd_ref) + p for p in mine]
        after_chunk.append(len(mine))

    def piece_copy(k):
        src, _, r0, nr, c0, nc = pieces[k]
        slot = k % n_slots
        return pltpu.make_async_copy(src.at[layer, pl.ds(r0, nr), pl.ds(c0, nc)],
                                     stage_ref.at[slot, pl.ds(0, nr), pl.ds(0, nc)], sem.at[slot])

    def fetch(k):
        _, dst, r0, nr, c0, nc = pieces[k]
        piece_copy(k).wait()
        dst[pl.ds(r0, nr), pl.ds(c0, nc)] = stage_ref[k % n_slots, 0:nr, 0:nc].astype(BF16)
        if k + n_slots < len(pieces):
            piece_copy(k + n_slots).start()

    def hidden(h, lo, width):
        gate = jnp.dot(h, wg_ref[:, lo:lo + width], preferred_element_type=F32)
        up = jnp.dot(h, wu_ref[:, lo:lo + width], preferred_element_type=F32)
        return (gate * _sigmoid(gate) * up).astype(BF16)

    def body(streaming):
        fetched = 0
        if streaming:
            for k in range(n_slots):
                piece_copy(k).start()
        h_parts = []
        for c, (lo, width) in enumerate(chunks):
            if streaming:
                fetch(fetched)
                fetch(fetched + 1)
                fetched += 2
            if c == 0:
                for j in range(parts):
                    part = pl.ds(j * rows, rows)
                    h_parts.append(_rms_norm(x_ref[part, :], g_ref[...]).astype(BF16))
                    a_ref[part, lo:lo + width] = hidden(h_parts[j], lo, width)
                h = jnp.concatenate(h_parts, axis=0)
            else:
                a_ref[:, lo:lo + width] = hidden(h, lo, width)
            if streaming:
                for _ in range(after_chunk[c]):
                    fetch(fetched)
                    fetched += 1
        y = x_ref[...] + 0.5 * jnp.dot(a_ref[...], wd_ref[...], preferred_element_type=F32)
        if final_norm:
            y = _rms_norm(y, fg_ref[...])
        o_ref[...] = y

    @pl.when(pl.program_id(0) == 0)
    def _():
        body(True)

    @pl.when(pl.program_id(0) > 0)
    def _():
        body(False)


def _ff_chunks(d_ff, chunk):
    out, lo = [], 0
    while lo < d_ff:
        w = min(chunk, d_ff - lo)
        out.append((lo, w))
        lo += w
    return tuple(out)


def _ffn(x, layer, norm_g, w_gate, w_up, w_down, final_g, *, final_norm, tm=1024, ff_chunk=512, parts=4):
    n, d = x.shape
    d_ff = w_gate.shape[2]
    hbm = pl.BlockSpec(memory_space=pl.ANY)
    return pl.pallas_call(
        functools.partial(_ffn_kernel, layer=layer, chunks=_ff_chunks(d_ff, ff_chunk), final_norm=final_norm,
                          parts=parts),
        grid=(n // tm,),
        in_specs=[pl.BlockSpec((tm, d), lambda i: (i, 0)),
                  _layer_spec(layer, 1, d),
                  hbm, hbm, hbm,
                  pl.BlockSpec((1, d), lambda i: (0, 0), pipeline_mode=pl.Buffered(1))],
        out_specs=pl.BlockSpec((tm, d), lambda i: (i, 0)),
        out_shape=jax.ShapeDtypeStruct((n, d), F32),
        scratch_shapes=[pltpu.VMEM((tm, d_ff), BF16),
                        pltpu.VMEM((d, d_ff), BF16),
                        pltpu.VMEM((d, d_ff), BF16),
                        pltpu.VMEM((d_ff, d), BF16),
                        pltpu.VMEM((FFN_STAGE_SLOTS, d, ff_chunk), F32),
                        pltpu.SemaphoreType.DMA((FFN_STAGE_SLOTS,))],
        compiler_params=pltpu.CompilerParams(
            dimension_semantics=("arbitrary",), vmem_limit_bytes=VMEM_LIMIT_BYTES),
        name="ffn",
    )(x, norm_g, w_gate, w_up, w_down, final_g)


def _mixer_kernel(sink_ref, x_ref, cos_ref, sin_ref, g_ref, win_ref, cw_ref, cb_ref, lng_ref, lnb_ref,
                  wout_ref, o_ref, kv_ref, hg_ref, sh_ref, *, tm, ts, layer):
    s_idx = pl.program_id(1)
    conv_c = cw_ref.shape[1]
    n_sub = tm // ts
    n_blocks = ts // BLOCK
    q_end, k_end, v_end = ATTN_WIDTH, ATTN_WIDTH + KV_WIDTH, ATTN_WIDTH + 2 * KV_WIDTH

    @pl.when(s_idx == 0)
    def _():
        kv_ref[:, 0:BLOCK, :] = jnp.zeros((8, BLOCK, LANES), BF16)
        hg_ref[0:CONV_HIST, :] = jnp.zeros((CONV_HIST, conv_c), F32)

    lane = lax.broadcasted_iota(jnp.int32, (1, LANES), 1)
    low_half = (lane & (HEAD_DIM - 1)) < HALF
    first = lane < HEAD_DIM
    zero = jnp.zeros((), F32)

    qi = lax.broadcasted_iota(jnp.int32, (BLOCK, 2 * BLOCK), 0)
    kj = lax.broadcasted_iota(jnp.int32, (BLOCK, 2 * BLOCK), 1)
    in_window = (kj > qi) & (kj <= qi + BLOCK)
    first_block_mask = in_window & ((kj >= BLOCK) | (s_idx > 0))
    neg = jnp.finfo(F32).min
    nt = (((1,), (1,)), ((), ()))

    def project(sub, out):
        lo = sub * ts
        tile_rows = pl.ds(lo, ts)
        x = x_ref[tile_rows, :]
        out["x"] = x
        h = _rms_norm(x, g_ref[...]).astype(BF16)
        cos = cos_ref[tile_rows, :]
        sin = sin_ref[tile_rows, :]

        def proj(lo_col, width):
            return jnp.dot(h, win_ref[:, lo_col:lo_col + width], preferred_element_type=F32)

        def rope(t):
            partner = jnp.where(low_half, pltpu.roll(t, LANES - HALF, 1), pltpu.roll(t, HALF, 1))
            return t * cos + partner * sin

        out["q"] = []
        for lo_col in range(0, q_end, MXU_COLS):
            pq = proj(lo_col, MXU_COLS)
            out["q"] += [rope(pq[:, i * LANES:(i + 1) * LANES]).astype(BF16) for i in range(MXU_COLS // LANES)]
            yield

        pkv = proj(q_end, 2 * KV_WIDTH)
        k = rope(pkv[:, :KV_WIDTH])
        v = pkv[:, KV_WIDTH:]
        rows = pl.ds(BLOCK + lo, ts)
        k_sw = pltpu.roll(k, HEAD_DIM, 1)
        v_sw = pltpu.roll(v, HEAD_DIM, 1)
        for i, t in enumerate((jnp.where(first, k, zero), jnp.where(first, zero, k_sw),
                               jnp.where(first, k_sw, zero), jnp.where(first, zero, k),
                               jnp.where(first, v, zero), jnp.where(first, zero, v_sw),
                               jnp.where(first, v_sw, zero), jnp.where(first, zero, v))):
            kv_ref[i, rows, :] = t.astype(BF16)
        yield

        n_rows = CONV_HIST + ts
        for lo_col in range(0, conv_c, MXU_COLS):
            a = proj(v_end + lo_col, MXU_COLS)
            yield
            gate = proj(v_end + conv_c + lo_col, MXU_COLS)
            cols = pl.ds(lo_col, MXU_COLS)
            hg_ref[pl.ds(CONV_HIST + lo, ts), cols] = a * _sigmoid(gate)
            hbuf = hg_ref[pl.ds(lo, n_rows), cols]
            for r in range(1, SUBLANES):
                sh_ref[sub % 2, r - 1, pl.ds(SUBLANES - r, n_rows), cols] = hbuf
            yield

    def conv_piece(sub, b, c):
        lo = sub * ts
        cols = pl.ds(c * LANES, LANES)
        acc = jnp.broadcast_to(cb_ref[:, cols], (BLOCK, LANES))
        for w in range(CONV_WIDTH):
            base, r = divmod(CONV_HIST - (CONV_WIDTH - 1) + w, SUBLANES)
            if r == 0:
                slab = hg_ref[pl.ds(lo + base * SUBLANES + b * BLOCK, BLOCK), cols]
            else:
                slab = sh_ref[sub % 2, r - 1, pl.ds((base + 1) * SUBLANES + b * BLOCK, BLOCK), cols]
            acc = acc + slab * cw_ref[w:w + 1, cols]
        return acc

    def mix(sub, proj_out):
        lo = sub * ts
        q = proj_out["q"]
        n_cols = ATTN_WIDTH // LANES
        units = [(c, b) for c in range(n_cols) for b in range(n_blocks)]
        conv = {}

        def conv_some(first_unit, count):
            for c, b in units[first_unit:first_unit + count]:
                conv[(c, b)] = conv_piece(sub, b, c)

        per_phase = -(-len(units) // 4)
        heads = [(c, b, e) for c, b in units for e in range(2)]
        kv_of = lambda c: c // (N_HEADS // N_KV_HEADS // 2)
        band_of = lambda b: pl.ds(lo + b * BLOCK, 2 * BLOCK)

        scores = {}
        for c, b, e in heads:
            qb = q[c][b * BLOCK:(b + 1) * BLOCK]
            scores[(c, b, e)] = lax.dot_general(qb, kv_ref[2 * kv_of(c) + e, band_of(b), :], nt,
                                                preferred_element_type=F32)
        conv_some(0, per_phase)
        yield

        row_max = {}
        for c, b, e in heads:
            mask = first_block_mask if lo + b == 0 else in_window
            scores[(c, b, e)] = jnp.where(mask, scores[(c, b, e)], neg)
            row_max[(c, b, e)] = jnp.maximum(jnp.max(scores[(c, b, e)], axis=-1, keepdims=True),
                                             sink_ref[layer, 2 * c + e])
        conv_some(per_phase, per_phase)
        yield

        probs, recip = {}, {}
        for c, b, e in heads:
            m = row_max[(c, b, e)]
            ex = jnp.exp(scores[(c, b, e)] - m)
            denom = jnp.sum(ex, axis=-1, keepdims=True) + jnp.exp(sink_ref[layer, 2 * c + e] - m)
            probs[(c, b, e)] = ex.astype(BF16)
            recip[(c, b, e)] = 1.0 / denom
        conv_some(2 * per_phase, per_phase)
        yield

        outs = {}
        for c, b in units:
            o = [jnp.dot(probs[(c, b, e)], kv_ref[4 + 2 * kv_of(c) + e, band_of(b), :],
                         preferred_element_type=F32) * recip[(c, b, e)] for e in range(2)]
            outs[(c, b)] = o[0] + o[1]
        conv_some(3 * per_phase, per_phase)
        yield

        gather = lambda d: jnp.concatenate(
            [jnp.concatenate([d[(c, b)] for b in range(n_blocks)], axis=0) for c in range(n_cols)], axis=1)
        attn_out = gather(outs).astype(BF16)
        acc = gather(conv)
        mu = jnp.mean(acc, axis=-1, keepdims=True)
        xc = acc - mu
        y = xc * lax.rsqrt(jnp.mean(xc * xc, axis=-1, keepdims=True) + EPS) * lng_ref[...] + lnb_ref[...]
        conv_out = (y * _sigmoid(y)).astype(BF16)
        mixed = jnp.concatenate([attn_out, conv_out], axis=1)
        o_ref[pl.ds(lo, ts), :] = proj_out["x"] + jnp.dot(mixed, wout_ref[...], preferred_element_type=F32)

    proj_out = {}
    for _ in project(0, proj_out):
        pass
    for sub in range(n_sub):
        next_out = {}
        stages = [mix(sub, proj_out)]
        if sub + 1 < n_sub:
            stages.append(project(sub + 1, next_out))
        while stages:
            for g in list(stages):
                if next(g, StopIteration) is StopIteration:
                    stages.remove(g)
        proj_out = next_out

    kv_ref[:, 0:BLOCK, :] = kv_ref[:, tm:tm + BLOCK, :]
    hg_ref[0:CONV_HIST, :] = hg_ref[tm:tm + CONV_HIST, :]


def _mixer(x, layer, cos, sin, norm_g, w_in, conv_w, conv_b, ln_g, ln_b, sinks, w_out, *, tm=1024, ts=256):
    B, S, d = x.shape
    d_in = w_in.shape[2]
    conv_c = conv_w.shape[2]
    d_mix = w_out.shape[1]
    assert conv_c == ATTN_WIDTH
    tile = lambda width: pl.BlockSpec((None, tm, width), lambda b, s: (b, s, 0))
    return pl.pallas_call(
        functools.partial(_mixer_kernel, tm=tm, ts=ts, layer=layer),
        grid=(B, S // tm),
        in_specs=[pl.BlockSpec(memory_space=pltpu.SMEM),
                  tile(d), tile(LANES), tile(LANES),
                  _layer_spec(layer, 1, d),
                  _layer_spec(layer, d, d_in),
                  _layer_spec(layer, CONV_WIDTH, conv_c),
                  _layer_spec(layer, 1, conv_c),
                  _layer_spec(layer, 1, conv_c),
                  _layer_spec(layer, 1, conv_c),
                  _layer_spec(layer, d_mix, d)],
        out_specs=tile(d),
        out_shape=jax.ShapeDtypeStruct((B, S, d), F32),
        scratch_shapes=[pltpu.VMEM((8, BLOCK + tm, LANES), BF16),
                        pltpu.VMEM((CONV_HIST + tm, conv_c), F32),
                        pltpu.VMEM((2, SUBLANES - 1, SUBLANES + CONV_HIST + ts, conv_c), F32)],
        compiler_params=pltpu.CompilerParams(
            dimension_semantics=("arbitrary", "arbitrary"), vmem_limit_bytes=VMEM_LIMIT_BYTES),
        name="mixer",
    )(sinks, x, cos, sin, norm_g, w_in, conv_w, conv_b, ln_g, ln_b, w_out)


def _prep_w_in(w_in):
    scale = jnp.where(jnp.arange(w_in.shape[2]) < ATTN_WIDTH, HEAD_DIM ** -0.5, 1.0).astype(F32)
    return (w_in * scale).astype(BF16)


def kernel(x, positions, ffn1_norm, ffn1_w_gate, ffn1_w_up, ffn1_w_down, mix_norm, w_in, conv_w, conv_b,
           conv_ln_g, conv_ln_b, attn_sinks, w_out, ffn2_norm, ffn2_w_gate, ffn2_w_up, ffn2_w_down,
           final_norm):
    B, S, d = x.shape
    depth = w_in.shape[0]
    row = lambda p: p.reshape(depth, 1, p.shape[-1])
    cos, sin = _rope_tables(positions)
    ffn1 = (row(ffn1_norm), ffn1_w_gate, ffn1_w_up, ffn1_w_down)
    ffn2 = (row(ffn2_norm), ffn2_w_gate, ffn2_w_up, ffn2_w_down)
    mix = (row(mix_norm), _prep_w_in(w_in), conv_w, row(conv_b), row(conv_ln_g), row(conv_ln_b),
           attn_sinks, w_out.astype(BF16))
    final_g = final_norm.reshape(1, d)
    x = x.reshape(B * S, d)
    for l in range(depth):
        x = _ffn(x, l, *ffn1, final_g, final_norm=False)
        x = _mixer(x.reshape(B, S, d), l, cos, sin, *mix).reshape(B * S, d)
        x = _ffn(x, l, *ffn2, final_g, final_norm=(l == depth - 1))
    return x.reshape(B, S, d)
```

```python
import functools

import jax
import jax.numpy as jnp
from jax import lax
from jax.experimental import pallas as pl
from jax.experimental.pallas import tpu as pltpu

HEAD_DIM = 64
N_HEADS = 8
N_KV_HEADS = 2
ATTN_WIDTH = N_HEADS * HEAD_DIM
KV_WIDTH = N_KV_HEADS * HEAD_DIM
CONV_WIDTH = 31
BLOCK = 128
ROPE_THETA = 10000.0
EPS = 1e-5

LANES = 128
SUBLANES = 8
PACK = 16
MXU_COLS = 256
VMEM_LIMIT_BYTES = 56 * 1024 * 1024

CONV_HIST = 32
HALF = HEAD_DIM // 2

BF16 = jnp.bfloat16
F32 = jnp.float32


def _sigmoid(x):
    return 1.0 / (1.0 + jnp.exp(-x))


def _rms_norm(x, g):
    return x * lax.rsqrt(jnp.mean(x * x, axis=-1, keepdims=True) + EPS) * g


def _layer_spec(layer, *shape):
    zeros = (0,) * len(shape)
    return pl.BlockSpec((None,) + shape, lambda *_: (layer,) + zeros, pipeline_mode=pl.Buffered(1))


def _rope_table_kernel(pos_ref, invf_ref, cos_ref, sin_ref):
    ang = pos_ref[...].astype(F32) * invf_ref[...]
    lane = lax.broadcasted_iota(jnp.int32, (1, LANES), 1)
    sign = jnp.where((lane & (HEAD_DIM - 1)) < HALF, -1.0, 1.0).astype(F32)
    for table, out_ref, scale in ((jnp.cos(ang), cos_ref, None), (jnp.sin(ang), sin_ref, sign)):
        for i in range(LANES // HALF):
            t = jnp.where((lane >= i * HALF) & (lane < (i + 1) * HALF), table, 0.0)
            t = t + pltpu.roll(t, 2 * HALF, 1)
            t = t + pltpu.roll(t, HALF, 1)
            out_ref[i] = t if scale is None else t * scale


def _rope_tables(positions):
    B, S = positions.shape
    n = B * S
    groups = LANES // HALF
    rows = n // groups
    inv_freq = 1.0 / (ROPE_THETA ** (jnp.arange(0, HEAD_DIM, 2, dtype=F32) / HEAD_DIM))
    invf = jnp.tile(inv_freq, groups).reshape(1, LANES)
    pos = jnp.repeat(positions.reshape(groups, rows).T, HALF, axis=1)
    tr = 512
    spec_out = pl.BlockSpec((groups, tr, LANES), lambda i: (0, i, 0))
    cos, sin = pl.pallas_call(
        _rope_table_kernel,
        grid=(rows // tr,),
        in_specs=[pl.BlockSpec((tr, LANES), lambda i: (i, 0)),
                  pl.BlockSpec((1, LANES), lambda i: (0, 0))],
        out_specs=[spec_out, spec_out],
        out_shape=[jax.ShapeDtypeStruct((groups, rows, LANES), F32)] * 2,
        name="rope_tables",
    )(pos, invf)
    return cos.reshape(B, S, LANES), sin.reshape(B, S, LANES)


def _ffn_kernel(*refs, chunks, final_norm, parts, n_cast):
    x_ref, g_ref, wg_ref, wu_ref, wd_ref, fg_ref = refs[:6]
    cast_in = refs[6:6 + n_cast]
    o_ref = refs[6 + n_cast]
    cast_out = refs[7 + n_cast:7 + 2 * n_cast]
    a_ref = refs[7 + 2 * n_cast]
    tm = x_ref.shape[0]
    rows = tm // parts

    def hidden(h, lo, width):
        gate = jnp.dot(h, wg_ref[:, lo:lo + width], preferred_element_type=F32)
        up = jnp.dot(h, wu_ref[:, lo:lo + width], preferred_element_type=F32)
        return (gate * _sigmoid(gate) * up).astype(BF16)

    h_parts = []
    lo, width = chunks[0]
    for j in range(parts):
        part = pl.ds(j * rows, rows)
        h_parts.append(_rms_norm(x_ref[part, :], g_ref[...]).astype(BF16))
        a_ref[part, lo:lo + width] = hidden(h_parts[j], lo, width)
    h = jnp.concatenate(h_parts, axis=0)
    for i, (lo, width) in enumerate(chunks[1:]):
        a_ref[:, lo:lo + width] = hidden(h, lo, width)
        if i < n_cast:
            cast_out[i][...] = cast_in[i][...].astype(BF16)
    y = x_ref[...] + 0.5 * jnp.dot(a_ref[...], wd_ref[...], preferred_element_type=F32)
    if final_norm:
        y = _rms_norm(y, fg_ref[...])
    o_ref[...] = y


def _ff_chunks(d_ff, chunk):
    out, lo = [], 0
    while lo < d_ff:
        w = min(chunk, d_ff - lo)
        out.append((lo, w))
        lo += w
    return tuple(out)


def _ffn(x, layer, norm_g, weights, final_g, *, final_norm, cast_next=None, tm=1024, ff_chunk=512, parts=4):
    n, d = x.shape
    w_gate, w_up, w_down = weights
    d_ff = w_gate.shape[2]
    steps = n // tm
    cast_arrays, cast_layer = cast_next if cast_next is not None else ((), 0)
    cast_in_specs, cast_out_specs, cast_shapes = [], [], []
    for w in cast_arrays:
        slab = w.shape[1] // steps
        assert slab * steps == w.shape[1] and slab % PACK == 0
        cast_in_specs.append(pl.BlockSpec((None, slab, w.shape[2]), lambda i: (cast_layer, i, 0)))
        cast_out_specs.append(pl.BlockSpec((None, slab, w.shape[2]), lambda i: (0, i, 0)))
        cast_shapes.append(jax.ShapeDtypeStruct((1,) + w.shape[1:], BF16))
    outs = pl.pallas_call(
        functools.partial(_ffn_kernel, chunks=_ff_chunks(d_ff, ff_chunk), final_norm=final_norm, parts=parts,
                          n_cast=len(cast_arrays)),
        grid=(steps,),
        in_specs=[pl.BlockSpec((tm, d), lambda i: (i, 0)),
                  _layer_spec(layer, 1, d),
                  _layer_spec(layer, d, d_ff),
                  _layer_spec(layer, d, d_ff),
                  _layer_spec(layer, d_ff, d),
                  pl.BlockSpec((1, d), lambda i: (0, 0), pipeline_mode=pl.Buffered(1))] + cast_in_specs,
        out_specs=[pl.BlockSpec((tm, d), lambda i: (i, 0))] + cast_out_specs,
        out_shape=[jax.ShapeDtypeStruct((n, d), F32)] + cast_shapes,
        scratch_shapes=[pltpu.VMEM((tm, d_ff), BF16)],
        compiler_params=pltpu.CompilerParams(
            dimension_semantics=("arbitrary",), vmem_limit_bytes=VMEM_LIMIT_BYTES),
        name="ffn",
    )(x, norm_g, w_gate, w_up, w_down, final_g, *cast_arrays)
    return outs[0], tuple(outs[1:])


def _mixer_kernel(sink_ref, x_ref, cos_ref, sin_ref, g_ref, win_ref, cw_ref, cb_ref, lng_ref, lnb_ref,
                  wout_ref, o_ref, kv_ref, hg_ref, sh_ref, *, tm, ts, layer):
    s_idx = pl.program_id(1)
    conv_c = cw_ref.shape[1]
    n_sub = tm // ts
    n_blocks = ts // BLOCK
    q_end, k_end, v_end = ATTN_WIDTH, ATTN_WIDTH + KV_WIDTH, ATTN_WIDTH + 2 * KV_WIDTH

    @pl.when(s_idx == 0)
    def _():
        kv_ref[:, 0:BLOCK, :] = jnp.zeros((8, BLOCK, LANES), BF16)
        hg_ref[0:CONV_HIST, :] = jnp.zeros((CONV_HIST, conv_c), F32)

    lane = lax.broadcasted_iota(jnp.int32, (1, LANES), 1)
    low_half = (lane & (HEAD_DIM - 1)) < HALF
    first = lane < HEAD_DIM
    zero = jnp.zeros((), F32)

    qi = lax.broadcasted_iota(jnp.int32, (BLOCK, 2 * BLOCK), 0)
    kj = lax.broadcasted_iota(jnp.int32, (BLOCK, 2 * BLOCK), 1)
    in_window = (kj > qi) & (kj <= qi + BLOCK)
    first_block_mask = in_window & ((kj >= BLOCK) | (s_idx > 0))
    neg = jnp.finfo(F32).min
    nt = (((1,), (1,)), ((), ()))

    def project(sub, out):
        lo = sub * ts
        tile_rows = pl.ds(lo, ts)
        x = x_ref[tile_rows, :]
        out["x"] = x
        h = _rms_norm(x, g_ref[...]).astype(BF16)
        cos = cos_ref[tile_rows, :]
        sin = sin_ref[tile_rows, :]

        def proj(lo_col, width):
            return jnp.dot(h, win_ref[:, lo_col:lo_col + width], preferred_element_type=F32)

        def rope(t):
            partner = jnp.where(low_half, pltpu.roll(t, LANES - HALF, 1), pltpu.roll(t, HALF, 1))
            return t * cos + partner * sin

        out["q"] = []
        for lo_col in range(0, q_end, MXU_COLS):
            pq = proj(lo_col, MXU_COLS)
            out["q"] += [rope(pq[:, i * LANES:(i + 1) * LANES]).astype(BF16) for i in range(MXU_COLS // LANES)]
            yield

        pkv = proj(q_end, 2 * KV_WIDTH)
        k = rope(pkv[:, :KV_WIDTH])
        v = pkv[:, KV_WIDTH:]
        rows = pl.ds(BLOCK + lo, ts)
        k_sw = pltpu.roll(k, HEAD_DIM, 1)
        v_sw = pltpu.roll(v, HEAD_DIM, 1)
        for i, t in enumerate((jnp.where(first, k, zero), jnp.where(first, zero, k_sw),
                               jnp.where(first, k_sw, zero), jnp.where(first, zero, k),
                               jnp.where(first, v, zero), jnp.where(first, zero, v_sw),
                               jnp.where(first, v_sw, zero), jnp.where(first, zero, v))):
            kv_ref[i, rows, :] = t.astype(BF16)
        yield

        n_rows = CONV_HIST + ts
        for lo_col in range(0, conv_c, MXU_COLS):
            a = proj(v_end + lo_col, MXU_COLS)
            yield
            gate = proj(v_end + conv_c + lo_col, MXU_COLS)
            cols = pl.ds(lo_col, MXU_COLS)
            hg_ref[pl.ds(CONV_HIST + lo, ts), cols] = a * _sigmoid(gate)
            hbuf = hg_ref[pl.ds(lo, n_rows), cols]
            for r in range(1, SUBLANES):
                sh_ref[sub % 2, r - 1, pl.ds(SUBLANES - r, n_rows), cols] = hbuf
            yield

    def conv_piece(sub, b, c):
        lo = sub * ts
        cols = pl.ds(c * LANES, LANES)
        acc = jnp.broadcast_to(cb_ref[:, cols], (BLOCK, LANES))
        for w in range(CONV_WIDTH):
            base, r = divmod(CONV_HIST - (CONV_WIDTH - 1) + w, SUBLANES)
            if r == 0:
                slab = hg_ref[pl.ds(lo + base * SUBLANES + b * BLOCK, BLOCK), cols]
            else:
                slab = sh_ref[sub % 2, r - 1, pl.ds((base + 1) * SUBLANES + b * BLOCK, BLOCK), cols]
            acc = acc + slab * cw_ref[w:w + 1, cols]
        return acc

    def mix(sub, proj_out):
        lo = sub * ts
        q = proj_out["q"]
        n_cols = ATTN_WIDTH // LANES
        units = [(c, b) for c in range(n_cols) for b in range(n_blocks)]
        conv = {}

        def conv_some(first_unit, count):
            for c, b in units[first_unit:first_unit + count]:
                conv[(c, b)] = conv_piece(sub, b, c)

        per_phase = -(-len(units) // 4)
        heads = [(c, b, e) for c, b in units for e in range(2)]
        kv_of = lambda c: c // (N_HEADS // N_KV_HEADS // 2)
        band_of = lambda b: pl.ds(lo + b * BLOCK, 2 * BLOCK)

        scores = {}
        for c, b, e in heads:
            qb = q[c][b * BLOCK:(b + 1) * BLOCK]
            scores[(c, b, e)] = lax.dot_general(qb, kv_ref[2 * kv_of(c) + e, band_of(b), :], nt,
                                                preferred_element_type=F32)
        conv_some(0, per_phase)
        yield

        row_max = {}
        for c, b, e in heads:
            mask = first_block_mask if lo + b == 0 else in_window
            scores[(c, b, e)] = jnp.where(mask, scores[(c, b, e)], neg)
            row_max[(c, b, e)] = jnp.maximum(jnp.max(scores[(c, b, e)], axis=-1, keepdims=True),
                                             sink_ref[layer, 2 * c + e])
        conv_some(per_phase, per_phase)
        yield

        probs, recip = {}, {}
        for c, b, e in heads:
            m = row_max[(c, b, e)]
            ex = jnp.exp(scores[(c, b, e)] - m)
            denom = jnp.sum(ex, axis=-1, keepdims=True) + jnp.exp(sink_ref[layer, 2 * c + e] - m)
            probs[(c, b, e)] = ex.astype(BF16)
            recip[(c, b, e)] = 1.0 / denom
        conv_some(2 * per_phase, per_phase)
        yield

        outs = {}
        for c, b in units:
            o = [jnp.dot(probs[(c, b, e)], kv_ref[4 + 2 * kv_of(c) + e, band_of(b), :],
                         preferred_element_type=F32) * recip[(c, b, e)] for e in range(2)]
            outs[(c, b)] = o[0] + o[1]
        conv_some(3 * per_phase, per_phase)
        yield

        gather = lambda d: jnp.concatenate(
            [jnp.concatenate([d[(c, b)] for b in range(n_blocks)], axis=0) for c in range(n_cols)], axis=1)
        attn_out = gather(outs).astype(BF16)
        acc = gather(conv)
        mu = jnp.mean(acc, axis=-1, keepdims=True)
        xc = acc - mu
        y = xc * lax.rsqrt(jnp.mean(xc * xc, axis=-1, keepdims=True) + EPS) * lng_ref[...] + lnb_ref[...]
        conv_out = (y * _sigmoid(y)).astype(BF16)
        mixed = jnp.concatenate([attn_out, conv_out], axis=1)
        o_ref[pl.ds(lo, ts), :] = proj_out["x"] + jnp.dot(mixed, wout_ref[...], preferred_element_type=F32)

    proj_out = {}
    for _ in project(0, proj_out):
        pass
    for sub in range(n_sub):
        next_out = {}
        stages = [mix(sub, proj_out)]
        if sub + 1 < n_sub:
            stages.append(project(sub + 1, next_out))
        while stages:
            for g in list(stages):
                if next(g, StopIteration) is StopIteration:
                    stages.remove(g)
        proj_out = next_out

    kv_ref[:, 0:BLOCK, :] = kv_ref[:, tm:tm + BLOCK, :]
    hg_ref[0:CONV_HIST, :] = hg_ref[tm:tm + CONV_HIST, :]


def _mixer(x, layer, cos, sin, norm_g, w_in, conv_w, conv_b, ln_g, ln_b, sinks, w_out, *, tm=1024, ts=256):
    B, S, d = x.shape
    d_in = w_in.shape[2]
    conv_c = conv_w.shape[2]
    d_mix = w_out.shape[1]
    assert conv_c == ATTN_WIDTH
    tile = lambda width: pl.BlockSpec((None, tm, width), lambda b, s: (b, s, 0))
    return pl.pallas_call(
        functools.partial(_mixer_kernel, tm=tm, ts=ts, layer=layer),
        grid=(B, S // tm),
        in_specs=[pl.BlockSpec(memory_space=pltpu.SMEM),
                  tile(d), tile(LANES), tile(LANES),
                  _layer_spec(layer, 1, d),
                  _layer_spec(layer, d, d_in),
                  _layer_spec(layer, CONV_WIDTH, conv_c),
                  _layer_spec(layer, 1, conv_c),
                  _layer_spec(layer, 1, conv_c),
                  _layer_spec(layer, 1, conv_c),
                  _layer_spec(layer, d_mix, d)],
        out_specs=tile(d),
        out_shape=jax.ShapeDtypeStruct((B, S, d), F32),
        scratch_shapes=[pltpu.VMEM((8, BLOCK + tm, LANES), BF16),
                        pltpu.VMEM((CONV_HIST + tm, conv_c), F32),
                        pltpu.VMEM((2, SUBLANES - 1, SUBLANES + CONV_HIST + ts, conv_c), F32)],
        compiler_params=pltpu.CompilerParams(
            dimension_semantics=("arbitrary", "arbitrary"), vmem_limit_bytes=VMEM_LIMIT_BYTES),
        name="mixer",
    )(sinks, x, cos, sin, norm_g, w_in, conv_w, conv_b, ln_g, ln_b, w_out)


def _prep_w_in(w_in):
    scale = jnp.where(jnp.arange(w_in.shape[2]) < ATTN_WIDTH, HEAD_DIM ** -0.5, 1.0).astype(F32)
    return (w_in * scale).astype(BF16)


def kernel(x, positions, ffn1_norm, ffn1_w_gate, ffn1_w_up, ffn1_w_down, mix_norm, w_in, conv_w, conv_b,
           conv_ln_g, conv_ln_b, attn_sinks, w_out, ffn2_norm, ffn2_w_gate, ffn2_w_up, ffn2_w_down,
           final_norm):
    B, S, d = x.shape
    depth = w_in.shape[0]
    row = lambda p: p.reshape(depth, 1, p.shape[-1])
    cos, sin = _rope_tables(positions)
    ffn1_f32 = (ffn1_w_gate, ffn1_w_up, ffn1_w_down)
    ffn2_f32 = (ffn2_w_gate, ffn2_w_up, ffn2_w_down)
    ffn1_g, ffn2_g = row(ffn1_norm), row(ffn2_norm)
    mix = (row(mix_norm), _prep_w_in(w_in), conv_w, row(conv_b), row(conv_ln_g), row(conv_ln_b),
           attn_sinks, w_out.astype(BF16))
    final_g = final_norm.reshape(1, d)
    x = x.reshape(B * S, d)
    weights = tuple(w[0:1].astype(BF16) for w in ffn1_f32)
    for l in range(depth):
        last = l == depth - 1
        x, weights = _ffn(x, 0, ffn1_g[l:l + 1], weights, final_g, final_norm=False, cast_next=(ffn2_f32, l))
        x = _mixer(x.reshape(B, S, d), l, cos, sin, *mix).reshape(B * S, d)
        x, weights = _ffn(x, 0, ffn2_g[l:l + 1], weights, final_g, final_norm=last,
                          cast_next=None if last else (ffn1_f32, l + 1))
    return x.reshape(B, S, d)
```

```python
import functools

import jax
import jax.numpy as jnp
from jax import lax
from jax.experimental import pallas as pl
from jax.experimental.pallas import tpu as pltpu

HEAD_DIM = 64
N_HEADS = 8
N_KV_HEADS = 2
ATTN_WIDTH = N_HEADS * HEAD_DIM
KV_WIDTH = N_KV_HEADS * HEAD_DIM
CONV_WIDTH = 31
BLOCK = 128
ROPE_THETA = 10000.0
EPS = 1e-5

LANES = 128
SUBLANES = 8
PACK = 16
MXU_COLS = 256
VMEM_LIMIT_BYTES = 56 * 1024 * 1024

CONV_HIST = 32
HALF = HEAD_DIM // 2

BF16 = jnp.bfloat16
F32 = jnp.float32


def _sigmoid(x):
    return 1.0 / (1.0 + jnp.exp(-x))


def _rms_norm(x, g):
    return x * lax.rsqrt(jnp.mean(x * x, axis=-1, keepdims=True) + EPS) * g


def _layer_spec(layer, *shape):
    zeros = (0,) * len(shape)
    return pl.BlockSpec((None,) + shape, lambda *_: (layer,) + zeros, pipeline_mode=pl.Buffered(1))


def _cast_specs(cast_next, steps):
    arrays, layer = cast_next
    in_specs, out_specs, shapes = [], [], []
    for w in arrays:
        slab = w.shape[1] // steps
        assert slab * steps == w.shape[1] and slab % PACK == 0
        in_specs.append(pl.BlockSpec((None, slab, w.shape[2]), lambda i: (layer, i, 0)))
        out_specs.append(pl.BlockSpec((None, slab, w.shape[2]), lambda i: (0, i, 0)))
        shapes.append(jax.ShapeDtypeStruct((1,) + w.shape[1:], BF16))
    return in_specs, out_specs, shapes


def _rope_table_kernel(pos_ref, invf_ref, *refs):
    n_cast = (len(refs) - 2) // 2
    cos_ref, sin_ref = refs[n_cast:n_cast + 2]
    for src, dst in zip(refs[:n_cast], refs[n_cast + 2:]):
        dst[...] = src[...].astype(BF16)
    ang = pos_ref[...].astype(F32) * invf_ref[...]
    lane = lax.broadcasted_iota(jnp.int32, (1, LANES), 1)
    sign = jnp.where((lane & (HEAD_DIM - 1)) < HALF, -1.0, 1.0).astype(F32)
    for table, out_ref, scale in ((jnp.cos(ang), cos_ref, None), (jnp.sin(ang), sin_ref, sign)):
        for i in range(LANES // HALF):
            t = jnp.where((lane >= i * HALF) & (lane < (i + 1) * HALF), table, 0.0)
            t = t + pltpu.roll(t, 2 * HALF, 1)
            t = t + pltpu.roll(t, HALF, 1)
            out_ref[i] = t if scale is None else t * scale


def _rope_tables(positions, cast_next):
    B, S = positions.shape
    n = B * S
    groups = LANES // HALF
    rows = n // groups
    inv_freq = 1.0 / (ROPE_THETA ** (jnp.arange(0, HEAD_DIM, 2, dtype=F32) / HEAD_DIM))
    invf = jnp.tile(inv_freq, groups).reshape(1, LANES)
    pos = jnp.repeat(positions.reshape(groups, rows).T, HALF, axis=1)
    tr = 512
    spec_out = pl.BlockSpec((groups, tr, LANES), lambda i: (0, i, 0))
    cast_in_specs, cast_out_specs, cast_shapes = _cast_specs(cast_next, rows // tr)
    cos, sin, *cast = pl.pallas_call(
        _rope_table_kernel,
        grid=(rows // tr,),
        in_specs=[pl.BlockSpec((tr, LANES), lambda i: (i, 0)),
                  pl.BlockSpec((1, LANES), lambda i: (0, 0))] + cast_in_specs,
        out_specs=[spec_out, spec_out] + cast_out_specs,
        out_shape=[jax.ShapeDtypeStruct((groups, rows, LANES), F32)] * 2 + cast_shapes,
        compiler_params=pltpu.CompilerParams(dimension_semantics=("arbitrary",),
                                             vmem_limit_bytes=VMEM_LIMIT_BYTES),
        name="rope_tables",
    )(pos, invf, *cast_next[0])
    return cos.reshape(B, S, LANES), sin.reshape(B, S, LANES), tuple(cast)


def _ffn_kernel(*refs, chunks, final_norm, parts, n_cast):
    x_ref, g_ref, wg_ref, wu_ref, wd_ref, fg_ref = refs[:6]
    cast_in = refs[6:6 + n_cast]
    o_ref = refs[6 + n_cast]
    cast_out = refs[7 + n_cast:7 + 2 * n_cast]
    a_ref = refs[7 + 2 * n_cast]
    tm = x_ref.shape[0]
    rows = tm // parts

    def hidden(h, lo, width):
        gate = jnp.dot(h, wg_ref[:, lo:lo + width], preferred_element_type=F32)
        up = jnp.dot(h, wu_ref[:, lo:lo + width], preferred_element_type=F32)
        return (gate * _sigmoid(gate) * up).astype(BF16)

    h_parts = []
    lo, width = chunks[0]
    for j in range(parts):
        part = pl.ds(j * rows, rows)
        h_parts.append(_rms_norm(x_ref[part, :], g_ref[...]).astype(BF16))
        a_ref[part, lo:lo + width] = hidden(h_parts[j], lo, width)
    h = jnp.concatenate(h_parts, axis=0)
    for i, (lo, width) in enumerate(chunks[1:]):
        a_ref[:, lo:lo + width] = hidden(h, lo, width)
        if i < n_cast:
            cast_out[i][...] = cast_in[i][...].astype(BF16)
    y = x_ref[...] + 0.5 * jnp.dot(a_ref[...], wd_ref[...], preferred_element_type=F32)
    if final_norm:
        y = _rms_norm(y, fg_ref[...])
    o_ref[...] = y


def _ff_chunks(d_ff, chunk):
    out, lo = [], 0
    while lo < d_ff:
        w = min(chunk, d_ff - lo)
        out.append((lo, w))
        lo += w
    return tuple(out)


def _ffn(x, layer, norm_g, weights, final_g, *, final_norm, cast_next=None, tm=1024, ff_chunk=512, parts=4):
    n, d = x.shape
    w_gate, w_up, w_down = weights
    d_ff = w_gate.shape[2]
    steps = n // tm
    cast_arrays = cast_next[0] if cast_next is not None else ()
    cast_in_specs, cast_out_specs, cast_shapes = _cast_specs(cast_next, steps) if cast_arrays else ([], [], [])
    assert len(cast_arrays) < len(_ff_chunks(d_ff, ff_chunk))
    outs = pl.pallas_call(
        functools.partial(_ffn_kernel, chunks=_ff_chunks(d_ff, ff_chunk), final_norm=final_norm, parts=parts,
                          n_cast=len(cast_arrays)),
        grid=(steps,),
        in_specs=[pl.BlockSpec((tm, d), lambda i: (i, 0)),
                  _layer_spec(layer, 1, d),
                  _layer_spec(layer, d, d_ff),
                  _layer_spec(layer, d, d_ff),
                  _layer_spec(layer, d_ff, d),
                  pl.BlockSpec((1, d), lambda i: (0, 0), pipeline_mode=pl.Buffered(1))] + cast_in_specs,
        out_specs=[pl.BlockSpec((tm, d), lambda i: (i, 0))] + cast_out_specs,
        out_shape=[jax.ShapeDtypeStruct((n, d), F32)] + cast_shapes,
        scratch_shapes=[pltpu.VMEM((tm, d_ff), BF16)],
        compiler_params=pltpu.CompilerParams(
            dimension_semantics=("arbitrary",), vmem_limit_bytes=VMEM_LIMIT_BYTES),
        name="ffn",
    )(x, norm_g, w_gate, w_up, w_down, final_g, *cast_arrays)
    return outs[0], tuple(outs[1:])


def _mixer_kernel(sink_ref, x_ref, cos_ref, sin_ref, g_ref, win_ref, cw_ref, cb_ref, lng_ref, lnb_ref,
                  wout_ref, o_ref, kv_ref, hg_ref, sh_ref, *, tm, ts, layer):
    s_idx = pl.program_id(1)
    conv_c = cw_ref.shape[1]
    n_sub = tm // ts
    n_blocks = ts // BLOCK
    q_end, k_end, v_end = ATTN_WIDTH, ATTN_WIDTH + KV_WIDTH, ATTN_WIDTH + 2 * KV_WIDTH

    @pl.when(s_idx == 0)
    def _():
        kv_ref[:, 0:BLOCK, :] = jnp.zeros((8, BLOCK, LANES), BF16)
        hg_ref[0:CONV_HIST, :] = jnp.zeros((CONV_HIST, conv_c), F32)

    lane = lax.broadcasted_iota(jnp.int32, (1, LANES), 1)
    low_half = (lane & (HEAD_DIM - 1)) < HALF
    first = lane < HEAD_DIM
    zero = jnp.zeros((), F32)

    qi = lax.broadcasted_iota(jnp.int32, (BLOCK, 2 * BLOCK), 0)
    kj = lax.broadcasted_iota(jnp.int32, (BLOCK, 2 * BLOCK), 1)
    in_window = (kj > qi) & (kj <= qi + BLOCK)
    first_block_mask = in_window & ((kj >= BLOCK) | (s_idx > 0))
    neg = jnp.finfo(F32).min
    nt = (((1,), (1,)), ((), ()))

    def project(sub, out):
        lo = sub * ts
        tile_rows = pl.ds(lo, ts)
        x = x_ref[tile_rows, :]
        out["x"] = x
        h = _rms_norm(x, g_ref[...]).astype(BF16)
        cos = cos_ref[tile_rows, :]
        sin = sin_ref[tile_rows, :]

        def proj(lo_col, width):
            return jnp.dot(h, win_ref[:, lo_col:lo_col + width], preferred_element_type=F32)

        def rope(t, cos, sin):
            partner = jnp.where(low_half, pltpu.roll(t, LANES - HALF, 1), pltpu.roll(t, HALF, 1))
            return t * cos + partner * sin

        cos_q, sin_q = cos * HEAD_DIM ** -0.5, sin * HEAD_DIM ** -0.5
        out["q"] = []
        for lo_col in range(0, q_end, MXU_COLS):
            pq = proj(lo_col, MXU_COLS)
            out["q"] += [rope(pq[:, i * LANES:(i + 1) * LANES], cos_q, sin_q).astype(BF16)
                         for i in range(MXU_COLS // LANES)]
            yield

        pkv = proj(q_end, 2 * KV_WIDTH)
        k = rope(pkv[:, :KV_WIDTH], cos, sin)
        v = pkv[:, KV_WIDTH:]
        rows = pl.ds(BLOCK + lo, ts)
        k_sw = pltpu.roll(k, HEAD_DIM, 1)
        v_sw = pltpu.roll(v, HEAD_DIM, 1)
        for i, t in enumerate((jnp.where(first, k, zero), jnp.where(first, zero, k_sw),
                               jnp.where(first, k_sw, zero), jnp.where(first, zero, k),
                               jnp.where(first, v, zero), jnp.where(first, zero, v_sw),
                               jnp.where(first, v_sw, zero), jnp.where(first, zero, v))):
            kv_ref[i, rows, :] = t.astype(BF16)
        yield

        n_rows = CONV_HIST + ts
        for lo_col in range(0, conv_c, MXU_COLS):
            a = proj(v_end + lo_col, MXU_COLS)
            yield
            gate = proj(v_end + conv_c + lo_col, MXU_COLS)
            cols = pl.ds(lo_col, MXU_COLS)
            hg_ref[pl.ds(CONV_HIST + lo, ts), cols] = a * _sigmoid(gate)
            hbuf = hg_ref[pl.ds(lo, n_rows), cols]
            for r in range(1, SUBLANES):
                sh_ref[sub % 2, r - 1, pl.ds(SUBLANES - r, n_rows), cols] = hbuf
            yield

    def conv_piece(sub, b, c):
        lo = sub * ts
        cols = pl.ds(c * LANES, LANES)
        acc = jnp.broadcast_to(cb_ref[:, cols], (BLOCK, LANES))
        for w in range(CONV_WIDTH):
            base, r = divmod(CONV_HIST - (CONV_WIDTH - 1) + w, SUBLANES)
            if r == 0:
                slab = hg_ref[pl.ds(lo + base * SUBLANES + b * BLOCK, BLOCK), cols]
            else:
                slab = sh_ref[sub % 2, r - 1, pl.ds((base + 1) * SUBLANES + b * BLOCK, BLOCK), cols]
            acc = acc + slab * cw_ref[w:w + 1, cols]
        return acc

    def mix(sub, proj_out):
        lo = sub * ts
        q = proj_out["q"]
        n_cols = ATTN_WIDTH // LANES
        units = [(c, b) for c in range(n_cols) for b in range(n_blocks)]
        conv = {}

        def conv_some(first_unit, count):
            for c, b in units[first_unit:first_unit + count]:
                conv[(c, b)] = conv_piece(sub, b, c)

        per_phase = -(-len(units) // 4)
        heads = [(c, b, e) for c, b in units for e in range(2)]
        kv_of = lambda c: c // (N_HEADS // N_KV_HEADS // 2)
        band_of = lambda b: pl.ds(lo + b * BLOCK, 2 * BLOCK)

        scores = {}
        for c, b, e in heads:
            qb = q[c][b * BLOCK:(b + 1) * BLOCK]
            scores[(c, b, e)] = lax.dot_general(qb, kv_ref[2 * kv_of(c) + e, band_of(b), :], nt,
                                                preferred_element_type=F32)
        conv_some(0, per_phase)
        yield

        row_max = {}
        for c, b, e in heads:
            mask = first_block_mask if lo + b == 0 else in_window
            scores[(c, b, e)] = jnp.where(mask, scores[(c, b, e)], neg)
            row_max[(c, b, e)] = jnp.maximum(jnp.max(scores[(c, b, e)], axis=-1, keepdims=True),
                                             sink_ref[layer, 2 * c + e])
        conv_some(per_phase, per_phase)
        yield

        probs, recip = {}, {}
        for c, b, e in heads:
            m = row_max[(c, b, e)]
            ex = jnp.exp(scores[(c, b, e)] - m)
            denom = jnp.sum(ex, axis=-1, keepdims=True) + jnp.exp(sink_ref[layer, 2 * c + e] - m)
            probs[(c, b, e)] = ex.astype(BF16)
            recip[(c, b, e)] = 1.0 / denom
        conv_some(2 * per_phase, per_phase)
        yield

        outs = {}
        for c, b in units:
            o = [jnp.dot(probs[(c, b, e)], kv_ref[4 + 2 * kv_of(c) + e, band_of(b), :],
                         preferred_element_type=F32) * recip[(c, b, e)] for e in range(2)]
            outs[(c, b)] = o[0] + o[1]
        conv_some(3 * per_phase, per_phase)
        yield

        gather = lambda d: jnp.concatenate(
            [jnp.concatenate([d[(c, b)] for b in range(n_blocks)], axis=0) for c in range(n_cols)], axis=1)
        attn_out = gather(outs).astype(BF16)
        acc = gather(conv)
        mu = jnp.mean(acc, axis=-1, keepdims=True)
        xc = acc - mu
        y = xc * lax.rsqrt(jnp.mean(xc * xc, axis=-1, keepdims=True) + EPS) * lng_ref[...] + lnb_ref[...]
        conv_out = (y * _sigmoid(y)).astype(BF16)
        mixed = jnp.concatenate([attn_out, conv_out], axis=1)
        o_ref[pl.ds(lo, ts), :] = proj_out["x"] + jnp.dot(mixed, wout_ref[...], preferred_element_type=F32)

    proj_out = {}
    for _ in project(0, proj_out):
        pass
    for sub in range(n_sub):
        next_out = {}
        stages = [mix(sub, proj_out)]
        if sub + 1 < n_sub:
            stages.append(project(sub + 1, next_out))
        while stages:
            for g in list(stages):
                if next(g, StopIteration) is StopIteration:
                    stages.remove(g)
        proj_out = next_out

    kv_ref[:, 0:BLOCK, :] = kv_ref[:, tm:tm + BLOCK, :]
    hg_ref[0:CONV_HIST, :] = hg_ref[tm:tm + CONV_HIST, :]


def _mixer(x, layer, cos, sin, norm_g, w_in, conv_w, conv_b, ln_g, ln_b, sinks, w_out, *, tm=1024, ts=256):
    B, S, d = x.shape
    d_in = w_in.shape[2]
    conv_c = conv_w.shape[2]
    d_mix = w_out.shape[1]
    assert conv_c == ATTN_WIDTH
    tile = lambda width: pl.BlockSpec((None, tm, width), lambda b, s: (b, s, 0))
    return pl.pallas_call(
        functools.partial(_mixer_kernel, tm=tm, ts=ts, layer=layer),
        grid=(B, S // tm),
        in_specs=[pl.BlockSpec(memory_space=pltpu.SMEM),
                  tile(d), tile(LANES), tile(LANES),
                  _layer_spec(layer, 1, d),
                  _layer_spec(0, d, d_in),
                  _layer_spec(layer, CONV_WIDTH, conv_c),
                  _layer_spec(layer, 1, conv_c),
                  _layer_spec(layer, 1, conv_c),
                  _layer_spec(layer, 1, conv_c),
                  _layer_spec(0, d_mix, d)],
        out_specs=tile(d),
        out_shape=jax.ShapeDtypeStruct((B, S, d), F32),
        scratch_shapes=[pltpu.VMEM((8, BLOCK + tm, LANES), BF16),
                        pltpu.VMEM((CONV_HIST + tm, conv_c), F32),
                        pltpu.VMEM((2, SUBLANES - 1, SUBLANES + CONV_HIST + ts, conv_c), F32)],
        compiler_params=pltpu.CompilerParams(
            dimension_semantics=("arbitrary", "arbitrary"), vmem_limit_bytes=VMEM_LIMIT_BYTES),
        name="mixer",
    )(sinks, x, cos, sin, norm_g, w_in, conv_w, conv_b, ln_g, ln_b, w_out)


def kernel(x, positions, ffn1_norm, ffn1_w_gate, ffn1_w_up, ffn1_w_down, mix_norm, w_in, conv_w, conv_b,
           conv_ln_g, conv_ln_b, attn_sinks, w_out, ffn2_norm, ffn2_w_gate, ffn2_w_up, ffn2_w_down,
           final_norm):
    B, S, d = x.shape
    depth = w_in.shape[0]
    row = lambda p: p.reshape(depth, 1, p.shape[-1])
    ffn1_f32 = (ffn1_w_gate, ffn1_w_up, ffn1_w_down)
    ffn2_f32 = (ffn2_w_gate, ffn2_w_up, ffn2_w_down)
    ffn1_g, ffn2_g = row(ffn1_norm), row(ffn2_norm)
    final_g = final_norm.reshape(1, d)
    cos, sin, weights = _rope_tables(positions, (ffn1_f32, 0))
    x = x.reshape(B * S, d)
    for l in range(depth):
        last = l == depth - 1
        x, cast = _ffn(x, 0, ffn1_g[l:l + 1], weights, final_g, final_norm=False,
                       cast_next=(ffn2_f32 + (w_in, w_out), l))
        weights, (w_in_l, w_out_l) = cast[:3], cast[3:]
        x = _mixer(x.reshape(B, S, d), l, cos, sin, row(mix_norm), w_in_l, conv_w, row(conv_b), row(conv_ln_g),
                   row(conv_ln_b), attn_sinks, w_out_l).reshape(B * S, d)
        x, weights = _ffn(x, 0, ffn2_g[l:l + 1], weights, final_g, final_norm=last,
                          cast_next=None if last else (ffn1_f32, l + 1))
    return x.reshape(B, S, d)
```

```python
import functools

import jax
import jax.numpy as jnp
from jax import lax
from jax.experimental import pallas as pl
from jax.experimental.pallas import tpu as pltpu

HEAD_DIM = 64
N_HEADS = 8
N_KV_HEADS = 2
ATTN_WIDTH = N_HEADS * HEAD_DIM
KV_WIDTH = N_KV_HEADS * HEAD_DIM
CONV_WIDTH = 31
BLOCK = 128
ROPE_THETA = 10000.0
EPS = 1e-5
LOG2E = 1.4426950408889634

LANES = 128
SUBLANES = 8
PACK = 16
MXU_COLS = 256
VMEM_LIMIT_BYTES = 56 * 1024 * 1024

CONV_HIST = 32
HALF = HEAD_DIM // 2

BF16 = jnp.bfloat16
F32 = jnp.float32


def _sigmoid(x):
    return 1.0 / (1.0 + jnp.exp(-x))


def _rms_norm(x, g):
    return x * lax.rsqrt(jnp.mean(x * x, axis=-1, keepdims=True) + EPS) * g


def _layer_spec(layer, *shape):
    zeros = (0,) * len(shape)
    return pl.BlockSpec((None,) + shape, lambda *_: (layer,) + zeros, pipeline_mode=pl.Buffered(1))


def _cast_specs(cast_next, steps):
    arrays, layer = cast_next
    in_specs, out_specs, shapes = [], [], []
    for w in arrays:
        slab = w.shape[1] // steps
        assert slab * steps == w.shape[1] and slab % PACK == 0
        in_specs.append(pl.BlockSpec((None, slab, w.shape[2]), lambda i: (layer, i, 0)))
        out_specs.append(pl.BlockSpec((None, slab, w.shape[2]), lambda i: (0, i, 0)))
        shapes.append(jax.ShapeDtypeStruct((1,) + w.shape[1:], BF16))
    return in_specs, out_specs, shapes


def _rope_table_kernel(pos_ref, invf_ref, *refs):
    n_cast = (len(refs) - 2) // 2
    cos_ref, sin_ref = refs[n_cast:n_cast + 2]
    for src, dst in zip(refs[:n_cast], refs[n_cast + 2:]):
        dst[...] = src[...].astype(BF16)
    ang = pos_ref[...].astype(F32) * invf_ref[...]
    lane = lax.broadcasted_iota(jnp.int32, (1, LANES), 1)
    sign = jnp.where((lane & (HEAD_DIM - 1)) < HALF, -1.0, 1.0).astype(F32)
    for table, out_ref, scale in ((jnp.cos(ang), cos_ref, None), (jnp.sin(ang), sin_ref, sign)):
        for i in range(LANES // HALF):
            t = jnp.where((lane >= i * HALF) & (lane < (i + 1) * HALF), table, 0.0)
            t = t + pltpu.roll(t, 2 * HALF, 1)
            t = t + pltpu.roll(t, HALF, 1)
            out_ref[i] = t if scale is None else t * scale


def _rope_tables(positions, cast_next):
    B, S = positions.shape
    n = B * S
    groups = LANES // HALF
    rows = n // groups
    inv_freq = 1.0 / (ROPE_THETA ** (jnp.arange(0, HEAD_DIM, 2, dtype=F32) / HEAD_DIM))
    invf = jnp.tile(inv_freq, groups).reshape(1, LANES)
    pos = jnp.repeat(positions.reshape(groups, rows).T, HALF, axis=1)
    tr = 512
    spec_out = pl.BlockSpec((groups, tr, LANES), lambda i: (0, i, 0))
    cast_in_specs, cast_out_specs, cast_shapes = _cast_specs(cast_next, rows // tr)
    cos, sin, *cast = pl.pallas_call(
        _rope_table_kernel,
        grid=(rows // tr,),
        in_specs=[pl.BlockSpec((tr, LANES), lambda i: (i, 0)),
                  pl.BlockSpec((1, LANES), lambda i: (0, 0))] + cast_in_specs,
        out_specs=[spec_out, spec_out] + cast_out_specs,
        out_shape=[jax.ShapeDtypeStruct((groups, rows, LANES), F32)] * 2 + cast_shapes,
        compiler_params=pltpu.CompilerParams(dimension_semantics=("arbitrary",),
                                             vmem_limit_bytes=VMEM_LIMIT_BYTES),
        name="rope_tables",
    )(pos, invf, *cast_next[0])
    return cos.reshape(B, S, LANES), sin.reshape(B, S, LANES), tuple(cast)


def _ffn_kernel(*refs, chunks, final_norm, parts, n_cast):
    x_ref, g_ref, wg_ref, wu_ref, wd_ref, fg_ref = refs[:6]
    cast_in = refs[6:6 + n_cast]
    o_ref = refs[6 + n_cast]
    cast_out = refs[7 + n_cast:7 + 2 * n_cast]
    a_ref = refs[7 + 2 * n_cast]
    tm = x_ref.shape[0]
    rows = tm // parts

    def hidden(h, lo, width):
        gate = jnp.dot(h, wg_ref[:, lo:lo + width], preferred_element_type=F32)
        up = jnp.dot(h, wu_ref[:, lo:lo + width], preferred_element_type=F32)
        return (gate * _sigmoid(gate) * up).astype(BF16)

    h_parts = []
    lo, width = chunks[0]
    for j in range(parts):
        part = pl.ds(j * rows, rows)
        h_parts.append(_rms_norm(x_ref[part, :], g_ref[...]).astype(BF16))
        a_ref[part, lo:lo + width] = hidden(h_parts[j], lo, width)
    h = jnp.concatenate(h_parts, axis=0)
    for i, (lo, width) in enumerate(chunks[1:]):
        a_ref[:, lo:lo + width] = hidden(h, lo, width)
        if i < n_cast:
            cast_out[i][...] = cast_in[i][...].astype(BF16)
    y = x_ref[...] + 0.5 * jnp.dot(a_ref[...], wd_ref[...], preferred_element_type=F32)
    if final_norm:
        y = _rms_norm(y, fg_ref[...])
    o_ref[...] = y


def _ff_chunks(d_ff, chunk):
    out, lo = [], 0
    while lo < d_ff:
        w = min(chunk, d_ff - lo)
        out.append((lo, w))
        lo += w
    return tuple(out)


def _ffn(x, layer, norm_g, weights, final_g, *, final_norm, cast_next=None, tm=1024, ff_chunk=512, parts=4):
    n, d = x.shape
    w_gate, w_up, w_down = weights
    d_ff = w_gate.shape[2]
    steps = n // tm
    cast_arrays = cast_next[0] if cast_next is not None else ()
    cast_in_specs, cast_out_specs, cast_shapes = _cast_specs(cast_next, steps) if cast_arrays else ([], [], [])
    assert len(cast_arrays) < len(_ff_chunks(d_ff, ff_chunk))
    outs = pl.pallas_call(
        functools.partial(_ffn_kernel, chunks=_ff_chunks(d_ff, ff_chunk), final_norm=final_norm, parts=parts,
                          n_cast=len(cast_arrays)),
        grid=(steps,),
        in_specs=[pl.BlockSpec((tm, d), lambda i: (i, 0)),
                  _layer_spec(layer, 1, d),
                  _layer_spec(layer, d, d_ff),
                  _layer_spec(layer, d, d_ff),
                  _layer_spec(layer, d_ff, d),
                  pl.BlockSpec((1, d), lambda i: (0, 0), pipeline_mode=pl.Buffered(1))] + cast_in_specs,
        out_specs=[pl.BlockSpec((tm, d), lambda i: (i, 0))] + cast_out_specs,
        out_shape=[jax.ShapeDtypeStruct((n, d), F32)] + cast_shapes,
        scratch_shapes=[pltpu.VMEM((tm, d_ff), BF16)],
        compiler_params=pltpu.CompilerParams(
            dimension_semantics=("arbitrary",), vmem_limit_bytes=VMEM_LIMIT_BYTES),
        name="ffn",
    )(x, norm_g, w_gate, w_up, w_down, final_g, *cast_arrays)
    return outs[0], tuple(outs[1:])


def _mixer_kernel(sink_ref, x_ref, cos_ref, sin_ref, g_ref, win_ref, cw_ref, cb_ref, lng_ref, lnb_ref,
                  wout_ref, o_ref, kv_ref, hg_ref, sh_ref, *, tm, ts, layer):
    s_idx = pl.program_id(1)
    conv_c = cw_ref.shape[1]
    n_sub = tm // ts
    n_blocks = ts // BLOCK
    q_end, k_end, v_end = ATTN_WIDTH, ATTN_WIDTH + KV_WIDTH, ATTN_WIDTH + 2 * KV_WIDTH

    @pl.when(s_idx == 0)
    def _():
        kv_ref[:, 0:BLOCK, :] = jnp.zeros((8, BLOCK, LANES), BF16)
        hg_ref[0:CONV_HIST, :] = jnp.zeros((CONV_HIST, conv_c), F32)

    lane = lax.broadcasted_iota(jnp.int32, (1, LANES), 1)
    low_half = (lane & (HEAD_DIM - 1)) < HALF
    first = lane < HEAD_DIM
    zero = jnp.zeros((), F32)

    qi = lax.broadcasted_iota(jnp.int32, (BLOCK, 2 * BLOCK), 0)
    kj = lax.broadcasted_iota(jnp.int32, (BLOCK, 2 * BLOCK), 1)
    in_window = (kj > qi) & (kj <= qi + BLOCK)
    first_block_mask = in_window & ((kj >= BLOCK) | (s_idx > 0))
    neg = jnp.finfo(F32).min
    nt = (((1,), (1,)), ((), ()))

    def project(sub, out):
        lo = sub * ts
        tile_rows = pl.ds(lo, ts)
        x = x_ref[tile_rows, :]
        out["x"] = x
        h = _rms_norm(x, g_ref[...]).astype(BF16)
        cos = cos_ref[tile_rows, :]
        sin = sin_ref[tile_rows, :]

        def proj(lo_col, width):
            return jnp.dot(h, win_ref[:, lo_col:lo_col + width], preferred_element_type=F32)

        def rope(t, cos, sin):
            partner = jnp.where(low_half, pltpu.roll(t, LANES - HALF, 1), pltpu.roll(t, HALF, 1))
            return t * cos + partner * sin

        cos_q, sin_q = cos * (HEAD_DIM ** -0.5 * LOG2E), sin * (HEAD_DIM ** -0.5 * LOG2E)
        out["q"] = []
        for lo_col in range(0, q_end, MXU_COLS):
            pq = proj(lo_col, MXU_COLS)
            out["q"] += [rope(pq[:, i * LANES:(i + 1) * LANES], cos_q, sin_q).astype(BF16)
                         for i in range(MXU_COLS // LANES)]
            yield

        pkv = proj(q_end, 2 * KV_WIDTH)
        k = rope(pkv[:, :KV_WIDTH], cos, sin)
        v = pkv[:, KV_WIDTH:]
        rows = pl.ds(BLOCK + lo, ts)
        k_sw = pltpu.roll(k, HEAD_DIM, 1)
        v_sw = pltpu.roll(v, HEAD_DIM, 1)
        for i, t in enumerate((jnp.where(first, k, zero), jnp.where(first, zero, k_sw),
                               jnp.where(first, k_sw, zero), jnp.where(first, zero, k),
                               jnp.where(first, v, zero), jnp.where(first, zero, v_sw),
                               jnp.where(first, v_sw, zero), jnp.where(first, zero, v))):
            kv_ref[i, rows, :] = t.astype(BF16)
        yield

        n_rows = CONV_HIST + ts
        for lo_col in range(0, conv_c, MXU_COLS):
            a = proj(v_end + lo_col, MXU_COLS)
            yield
            gate = proj(v_end + conv_c + lo_col, MXU_COLS)
            cols = pl.ds(lo_col, MXU_COLS)
            hg_ref[pl.ds(CONV_HIST + lo, ts), cols] = a * _sigmoid(gate)
            hbuf = hg_ref[pl.ds(lo, n_rows), cols]
            for r in range(1, SUBLANES):
                sh_ref[sub % 2, r - 1, pl.ds(SUBLANES - r, n_rows), cols] = hbuf
            yield

    def conv_piece(sub, b, c):
        lo = sub * ts
        cols = pl.ds(c * LANES, LANES)
        acc = jnp.broadcast_to(cb_ref[:, cols], (BLOCK, LANES))
        for w in range(CONV_WIDTH):
            base, r = divmod(CONV_HIST - (CONV_WIDTH - 1) + w, SUBLANES)
            if r == 0:
                slab = hg_ref[pl.ds(lo + base * SUBLANES + b * BLOCK, BLOCK), cols]
            else:
                slab = sh_ref[sub % 2, r - 1, pl.ds((base + 1) * SUBLANES + b * BLOCK, BLOCK), cols]
            acc = acc + slab * cw_ref[w:w + 1, cols]
        return acc

    def mix(sub, proj_out):
        lo = sub * ts
        q = proj_out["q"]
        n_cols = ATTN_WIDTH // LANES
        units = [(c, b) for c in range(n_cols) for b in range(n_blocks)]
        conv = {}

        def conv_some(first_unit, count):
            for c, b in units[first_unit:first_unit + count]:
                conv[(c, b)] = conv_piece(sub, b, c)

        per_phase = -(-len(units) // 4)
        heads = [(c, b, e) for c, b in units for e in range(2)]
        kv_of = lambda c: c // (N_HEADS // N_KV_HEADS // 2)
        band_of = lambda b: pl.ds(lo + b * BLOCK, 2 * BLOCK)

        scores = {}
        for c, b, e in heads:
            qb = q[c][b * BLOCK:(b + 1) * BLOCK]
            scores[(c, b, e)] = lax.dot_general(qb, kv_ref[2 * kv_of(c) + e, band_of(b), :], nt,
                                                preferred_element_type=F32)
        conv_some(0, per_phase)
        yield

        row_max = {}
        for c, b, e in heads:
            mask = first_block_mask if lo + b == 0 else in_window
            scores[(c, b, e)] = jnp.where(mask, scores[(c, b, e)], neg)
            row_max[(c, b, e)] = jnp.maximum(jnp.max(scores[(c, b, e)], axis=-1, keepdims=True),
                                             sink_ref[layer, 2 * c + e] * LOG2E)
        conv_some(per_phase, per_phase)
        yield

        probs, recip = {}, {}
        for c, b, e in heads:
            m = row_max[(c, b, e)]
            ex = jnp.exp2(scores[(c, b, e)] - m)
            denom = jnp.sum(ex, axis=-1, keepdims=True) + jnp.exp2(sink_ref[layer, 2 * c + e] * LOG2E - m)
            probs[(c, b, e)] = ex.astype(BF16)
            recip[(c, b, e)] = 1.0 / denom
        conv_some(2 * per_phase, per_phase)
        yield

        outs = {}
        for c, b in units:
            o = [jnp.dot(probs[(c, b, e)], kv_ref[4 + 2 * kv_of(c) + e, band_of(b), :],
                         preferred_element_type=F32) * recip[(c, b, e)] for e in range(2)]
            outs[(c, b)] = o[0] + o[1]
        conv_some(3 * per_phase, per_phase)
        yield

        gather = lambda d: jnp.concatenate(
            [jnp.concatenate([d[(c, b)] for b in range(n_blocks)], axis=0) for c in range(n_cols)], axis=1)
        attn_out = gather(outs).astype(BF16)
        acc = gather(conv)
        mu = jnp.mean(acc, axis=-1, keepdims=True)
        xc = acc - mu
        y = xc * lax.rsqrt(jnp.mean(xc * xc, axis=-1, keepdims=True) + EPS) * lng_ref[...] + lnb_ref[...]
        conv_out = (y * _sigmoid(y)).astype(BF16)
        mixed = jnp.concatenate([attn_out, conv_out], axis=1)
        o_ref[pl.ds(lo, ts), :] = proj_out["x"] + jnp.dot(mixed, wout_ref[...], preferred_element_type=F32)

    proj_out = {}
    for _ in project(0, proj_out):
        pass
    for sub in range(n_sub):
        next_out = {}
        stages = [mix(sub, proj_out)]
        if sub + 1 < n_sub:
            stages.append(project(sub + 1, next_out))
        while stages:
            for g in list(stages):
                if next(g, StopIteration) is StopIteration:
                    stages.remove(g)
        proj_out = next_out

    kv_ref[:, 0:BLOCK, :] = kv_ref[:, tm:tm + BLOCK, :]
    hg_ref[0:CONV_HIST, :] = hg_ref[tm:tm + CONV_HIST, :]


def _mixer(x, layer, cos, sin, norm_g, w_in, conv_w, conv_b, ln_g, ln_b, sinks, w_out, *, tm=1024, ts=256):
    B, S, d = x.shape
    d_in = w_in.shape[2]
    conv_c = conv_w.shape[2]
    d_mix = w_out.shape[1]
    assert conv_c == ATTN_WIDTH
    tile = lambda width: pl.BlockSpec((None, tm, width), lambda b, s: (b, s, 0))
    return pl.pallas_call(
        functools.partial(_mixer_kernel, tm=tm, ts=ts, layer=layer),
        grid=(B, S // tm),
        in_specs=[pl.BlockSpec(memory_space=pltpu.SMEM),
                  tile(d), tile(LANES), tile(LANES),
                  _layer_spec(layer, 1, d),
                  _layer_spec(0, d, d_in),
                  _layer_spec(layer, CONV_WIDTH, conv_c),
                  _layer_spec(layer, 1, conv_c),
                  _layer_spec(layer, 1, conv_c),
                  _layer_spec(layer, 1, conv_c),
                  _layer_spec(0, d_mix, d)],
        out_specs=tile(d),
        out_shape=jax.ShapeDtypeStruct((B, S, d), F32),
        scratch_shapes=[pltpu.VMEM((8, BLOCK + tm, LANES), BF16),
                        pltpu.VMEM((CONV_HIST + tm, conv_c), F32),
                        pltpu.VMEM((2, SUBLANES - 1, SUBLANES + CONV_HIST + ts, conv_c), F32)],
        compiler_params=pltpu.CompilerParams(
            dimension_semantics=("arbitrary", "arbitrary"), vmem_limit_bytes=VMEM_LIMIT_BYTES),
        name="mixer",
    )(sinks, x, cos, sin, norm_g, w_in, conv_w, conv_b, ln_g, ln_b, w_out)


def kernel(x, positions, ffn1_norm, ffn1_w_gate, ffn1_w_up, ffn1_w_down, mix_norm, w_in, conv_w, conv_b,
           conv_ln_g, conv_ln_b, attn_sinks, w_out, ffn2_norm, ffn2_w_gate, ffn2_w_up, ffn2_w_down,
           final_norm):
    B, S, d = x.shape
    depth = w_in.shape[0]
    row = lambda p: p.reshape(depth, 1, p.shape[-1])
    ffn1_f32 = (ffn1_w_gate, ffn1_w_up, ffn1_w_down)
    ffn2_f32 = (ffn2_w_gate, ffn2_w_up, ffn2_w_down)
    ffn1_g, ffn2_g = row(ffn1_norm), row(ffn2_norm)
    final_g = final_norm.reshape(1, d)
    cos, sin, weights = _rope_tables(positions, (ffn1_f32, 0))
    x = x.reshape(B * S, d)
    for l in range(depth):
        last = l == depth - 1
        x, cast = _ffn(x, 0, ffn1_g[l:l + 1], weights, final_g, final_norm=False,
                       cast_next=(ffn2_f32 + (w_in, w_out), l))
        weights, (w_in_l, w_out_l) = cast[:3], cast[3:]
        x = _mixer(x.reshape(B, S, d), l, cos, sin, row(mix_norm), w_in_l, conv_w, row(conv_b), row(conv_ln_g),
                   row(conv_ln_b), attn_sinks, w_out_l).reshape(B * S, d)
        x, weights = _ffn(x, 0, ffn2_g[l:l + 1], weights, final_g, final_norm=last,
                          cast_next=None if last else (ffn1_f32, l + 1))
    return x.reshape(B, S, d)
```

```python
import functools

import jax
import jax.numpy as jnp
from jax import lax
from jax.experimental import pallas as pl
from jax.experimental.pallas import tpu as pltpu

HEAD_DIM = 64
N_HEADS = 8
N_KV_HEADS = 2
ATTN_WIDTH = N_HEADS * HEAD_DIM
KV_WIDTH = N_KV_HEADS * HEAD_DIM
CONV_WIDTH = 31
BLOCK = 128
ROPE_THETA = 10000.0
EPS = 1e-5
LOG2E = 1.4426950408889634

LANES = 128
SUBLANES = 8
PACK = 16
MXU_COLS = 256
VMEM_LIMIT_BYTES = 56 * 1024 * 1024

CONV_HIST = 32
HALF = HEAD_DIM // 2

BF16 = jnp.bfloat16
F32 = jnp.float32


def _sigmoid(x):
    return 1.0 / (1.0 + jnp.exp(-x))


def _rms_norm(x, g):
    return x * lax.rsqrt(jnp.mean(x * x, axis=-1, keepdims=True) + EPS) * g


def _layer_spec(layer, *shape):
    zeros = (0,) * len(shape)
    return pl.BlockSpec((None,) + shape, lambda *_: (layer,) + zeros, pipeline_mode=pl.Buffered(1))


def _cast_specs(cast_next, steps):
    arrays, layer = cast_next
    in_specs, out_specs, shapes = [], [], []
    for w in arrays:
        slab = w.shape[1] // steps
        assert slab * steps == w.shape[1] and slab % PACK == 0
        in_specs.append(pl.BlockSpec((None, slab, w.shape[2]), lambda i: (layer, i, 0)))
        out_specs.append(pl.BlockSpec((None, slab, w.shape[2]), lambda i: (0, i, 0)))
        shapes.append(jax.ShapeDtypeStruct((1,) + w.shape[1:], BF16))
    return in_specs, out_specs, shapes


def _rope_table_kernel(pos_ref, invf_ref, *refs):
    n_cast = (len(refs) - 2) // 2
    cos_ref, sin_ref = refs[n_cast:n_cast + 2]
    for src, dst in zip(refs[:n_cast], refs[n_cast + 2:]):
        dst[...] = src[...].astype(BF16)
    ang = pos_ref[...].astype(F32) * invf_ref[...]
    lane = lax.broadcasted_iota(jnp.int32, (1, LANES), 1)
    sign = jnp.where((lane & (HEAD_DIM - 1)) < HALF, -1.0, 1.0).astype(F32)
    for table, out_ref, scale in ((jnp.cos(ang), cos_ref, None), (jnp.sin(ang), sin_ref, sign)):
        for i in range(LANES // HALF):
            t = jnp.where((lane >= i * HALF) & (lane < (i + 1) * HALF), table, 0.0)
            t = t + pltpu.roll(t, 2 * HALF, 1)
            t = t + pltpu.roll(t, HALF, 1)
            out_ref[i] = t if scale is None else t * scale


def _rope_tables(positions, cast_next):
    B, S = positions.shape
    n = B * S
    groups = LANES // HALF
    rows = n // groups
    inv_freq = 1.0 / (ROPE_THETA ** (jnp.arange(0, HEAD_DIM, 2, dtype=F32) / HEAD_DIM))
    invf = jnp.tile(inv_freq, groups).reshape(1, LANES)
    pos = jnp.repeat(positions.reshape(groups, rows).T, HALF, axis=1)
    tr = 512
    spec_out = pl.BlockSpec((groups, tr, LANES), lambda i: (0, i, 0))
    cast_in_specs, cast_out_specs, cast_shapes = _cast_specs(cast_next, rows // tr)
    cos, sin, *cast = pl.pallas_call(
        _rope_table_kernel,
        grid=(rows // tr,),
        in_specs=[pl.BlockSpec((tr, LANES), lambda i: (i, 0)),
                  pl.BlockSpec((1, LANES), lambda i: (0, 0))] + cast_in_specs,
        out_specs=[spec_out, spec_out] + cast_out_specs,
        out_shape=[jax.ShapeDtypeStruct((groups, rows, LANES), F32)] * 2 + cast_shapes,
        compiler_params=pltpu.CompilerParams(dimension_semantics=("arbitrary",),
                                             vmem_limit_bytes=VMEM_LIMIT_BYTES),
        name="rope_tables",
    )(pos, invf, *cast_next[0])
    return cos.reshape(B, S, LANES), sin.reshape(B, S, LANES), tuple(cast)


def _ffn_kernel(*refs, chunks, final_norm, parts, n_cast):
    x_ref, g_ref, wg_ref, wu_ref, wd_ref, fg_ref = refs[:6]
    cast_in = refs[6:6 + n_cast]
    o_ref = refs[6 + n_cast]
    cast_out = refs[7 + n_cast:7 + 2 * n_cast]
    a_ref = refs[7 + 2 * n_cast]
    tm = x_ref.shape[0]
    rows = tm // parts

    def hidden(h, lo, width):
        gate = jnp.dot(h, wg_ref[:, lo:lo + width], preferred_element_type=F32)
        up = jnp.dot(h, wu_ref[:, lo:lo + width], preferred_element_type=F32)
        return (gate * _sigmoid(gate) * up).astype(BF16)

    h_parts = []
    lo, width = chunks[0]
    for j in range(parts):
        part = pl.ds(j * rows, rows)
        h_parts.append(_rms_norm(x_ref[part, :], g_ref[...]).astype(BF16))
        a_ref[part, lo:lo + width] = hidden(h_parts[j], lo, width)
    h = jnp.concatenate(h_parts, axis=0)
    for i, (lo, width) in enumerate(chunks[1:]):
        a_ref[:, lo:lo + width] = hidden(h, lo, width)
        if i < n_cast:
            cast_out[i][...] = cast_in[i][...].astype(BF16)
    y = x_ref[...] + 0.5 * jnp.dot(a_ref[...], wd_ref[...], preferred_element_type=F32)
    if final_norm:
        y = _rms_norm(y, fg_ref[...])
    o_ref[...] = y


def _ff_chunks(d_ff, chunk):
    out, lo = [], 0
    while lo < d_ff:
        w = min(chunk, d_ff - lo)
        out.append((lo, w))
        lo += w
    return tuple(out)


def _ffn(x, layer, norm_g, weights, final_g, *, final_norm, cast_next=None, tm=1024, ff_chunk=256, parts=4):
    n, d = x.shape
    w_gate, w_up, w_down = weights
    d_ff = w_gate.shape[2]
    steps = n // tm
    cast_arrays = cast_next[0] if cast_next is not None else ()
    cast_in_specs, cast_out_specs, cast_shapes = _cast_specs(cast_next, steps) if cast_arrays else ([], [], [])
    assert len(cast_arrays) < len(_ff_chunks(d_ff, ff_chunk))
    outs = pl.pallas_call(
        functools.partial(_ffn_kernel, chunks=_ff_chunks(d_ff, ff_chunk), final_norm=final_norm, parts=parts,
                          n_cast=len(cast_arrays)),
        grid=(steps,),
        in_specs=[pl.BlockSpec((tm, d), lambda i: (i, 0)),
                  _layer_spec(layer, 1, d),
                  _layer_spec(layer, d, d_ff),
                  _layer_spec(layer, d, d_ff),
                  _layer_spec(layer, d_ff, d),
                  pl.BlockSpec((1, d), lambda i: (0, 0), pipeline_mode=pl.Buffered(1))] + cast_in_specs,
        out_specs=[pl.BlockSpec((tm, d), lambda i: (i, 0))] + cast_out_specs,
        out_shape=[jax.ShapeDtypeStruct((n, d), F32)] + cast_shapes,
        scratch_shapes=[pltpu.VMEM((tm, d_ff), BF16)],
        compiler_params=pltpu.CompilerParams(
            dimension_semantics=("arbitrary",), vmem_limit_bytes=VMEM_LIMIT_BYTES),
        name="ffn",
    )(x, norm_g, w_gate, w_up, w_down, final_g, *cast_arrays)
    return outs[0], tuple(outs[1:])


def _mixer_kernel(sink_ref, x_ref, cos_ref, sin_ref, g_ref, win_ref, cw_ref, cb_ref, lng_ref, lnb_ref,
                  wout_ref, o_ref, kv_ref, hg_ref, sh_ref, *, tm, ts, layer):
    s_idx = pl.program_id(1)
    conv_c = cw_ref.shape[1]
    n_sub = tm // ts
    n_blocks = ts // BLOCK
    q_end, k_end, v_end = ATTN_WIDTH, ATTN_WIDTH + KV_WIDTH, ATTN_WIDTH + 2 * KV_WIDTH

    @pl.when(s_idx == 0)
    def _():
        kv_ref[:, 0:BLOCK, :] = jnp.zeros((8, BLOCK, LANES), BF16)
        hg_ref[0:CONV_HIST, :] = jnp.zeros((CONV_HIST, conv_c), F32)

    lane = lax.broadcasted_iota(jnp.int32, (1, LANES), 1)
    low_half = (lane & (HEAD_DIM - 1)) < HALF
    first = lane < HEAD_DIM
    zero = jnp.zeros((), F32)

    qi = lax.broadcasted_iota(jnp.int32, (BLOCK, 2 * BLOCK), 0)
    kj = lax.broadcasted_iota(jnp.int32, (BLOCK, 2 * BLOCK), 1)
    in_window = (kj > qi) & (kj <= qi + BLOCK)
    first_block_mask = in_window & ((kj >= BLOCK) | (s_idx > 0))
    neg = jnp.finfo(F32).min
    nt = (((1,), (1,)), ((), ()))

    def project(sub, out):
        lo = sub * ts
        tile_rows = pl.ds(lo, ts)
        x = x_ref[tile_rows, :]
        out["x"] = x
        h = _rms_norm(x, g_ref[...]).astype(BF16)
        cos = cos_ref[tile_rows, :]
        sin = sin_ref[tile_rows, :]

        def proj(lo_col, width):
            return jnp.dot(h, win_ref[:, lo_col:lo_col + width], preferred_element_type=F32)

        def rope(t, cos, sin):
            partner = jnp.where(low_half, pltpu.roll(t, LANES - HALF, 1), pltpu.roll(t, HALF, 1))
            return t * cos + partner * sin

        cos_q, sin_q = cos * (HEAD_DIM ** -0.5 * LOG2E), sin * (HEAD_DIM ** -0.5 * LOG2E)
        out["q"] = []
        for lo_col in range(0, q_end, MXU_COLS):
            pq = proj(lo_col, MXU_COLS)
            out["q"] += [rope(pq[:, i * LANES:(i + 1) * LANES], cos_q, sin_q).astype(BF16)
                         for i in range(MXU_COLS // LANES)]
            yield

        pkv = proj(q_end, 2 * KV_WIDTH)
        k = rope(pkv[:, :KV_WIDTH], cos, sin)
        v = pkv[:, KV_WIDTH:]
        rows = pl.ds(BLOCK + lo, ts)
        k_sw = pltpu.roll(k, HEAD_DIM, 1)
        v_sw = pltpu.roll(v, HEAD_DIM, 1)
        for i, t in enumerate((jnp.where(first, k, zero), jnp.where(first, zero, k_sw),
                               jnp.where(first, k_sw, zero), jnp.where(first, zero, k),
                               jnp.where(first, v, zero), jnp.where(first, zero, v_sw),
                               jnp.where(first, v_sw, zero), jnp.where(first, zero, v))):
            kv_ref[i, rows, :] = t.astype(BF16)
        yield

        n_rows = CONV_HIST + ts
        for lo_col in range(0, conv_c, MXU_COLS):
            a = proj(v_end + lo_col, MXU_COLS)
            yield
            gate = proj(v_end + conv_c + lo_col, MXU_COLS)
            cols = pl.ds(lo_col, MXU_COLS)
            hg_ref[pl.ds(CONV_HIST + lo, ts), cols] = a * _sigmoid(gate)
            hbuf = hg_ref[pl.ds(lo, n_rows), cols]
            for r in range(1, SUBLANES):
                sh_ref[sub % 2, r - 1, pl.ds(SUBLANES - r, n_rows), cols] = hbuf
            yield

    def conv_piece(sub, b, c):
        lo = sub * ts
        cols = pl.ds(c * LANES, LANES)
        acc = jnp.broadcast_to(cb_ref[:, cols], (BLOCK, LANES))
        for w in range(CONV_WIDTH):
            base, r = divmod(CONV_HIST - (CONV_WIDTH - 1) + w, SUBLANES)
            if r == 0:
                slab = hg_ref[pl.ds(lo + base * SUBLANES + b * BLOCK, BLOCK), cols]
            else:
                slab = sh_ref[sub % 2, r - 1, pl.ds((base + 1) * SUBLANES + b * BLOCK, BLOCK), cols]
            acc = acc + slab * cw_ref[w:w + 1, cols]
        return acc

    def mix(sub, proj_out):
        lo = sub * ts
        q = proj_out["q"]
        n_cols = ATTN_WIDTH // LANES
        units = [(c, b) for c in range(n_cols) for b in range(n_blocks)]
        conv = {}

        def conv_some(first_unit, count):
            for c, b in units[first_unit:first_unit + count]:
                conv[(c, b)] = conv_piece(sub, b, c)

        per_phase = -(-len(units) // 4)
        heads = [(c, b, e) for c, b in units for e in range(2)]
        kv_of = lambda c: c // (N_HEADS // N_KV_HEADS // 2)
        band_of = lambda b: pl.ds(lo + b * BLOCK, 2 * BLOCK)

        scores = {}
        for c, b, e in heads:
            qb = q[c][b * BLOCK:(b + 1) * BLOCK]
            scores[(c, b, e)] = lax.dot_general(qb, kv_ref[2 * kv_of(c) + e, band_of(b), :], nt,
                                                preferred_element_type=F32)
        conv_some(0, per_phase)
        yield

        row_max = {}
        for c, b, e in heads:
            mask = first_block_mask if lo + b == 0 else in_window
            scores[(c, b, e)] = jnp.where(mask, scores[(c, b, e)], neg)
            row_max[(c, b, e)] = jnp.maximum(jnp.max(scores[(c, b, e)], axis=-1, keepdims=True),
                                             sink_ref[layer, 2 * c + e] * LOG2E)
        conv_some(per_phase, per_phase)
        yield

        probs, recip = {}, {}
        for c, b, e in heads:
            m = row_max[(c, b, e)]
            ex = jnp.exp2(scores[(c, b, e)] - m)
            denom = jnp.sum(ex, axis=-1, keepdims=True) + jnp.exp2(sink_ref[layer, 2 * c + e] * LOG2E - m)
            probs[(c, b, e)] = ex.astype(BF16)
            recip[(c, b, e)] = 1.0 / denom
        conv_some(2 * per_phase, per_phase)
        yield

        outs = {}
        for c, b in units:
            o = [jnp.dot(probs[(c, b, e)], kv_ref[4 + 2 * kv_of(c) + e, band_of(b), :],
                         preferred_element_type=F32) * recip[(c, b, e)] for e in range(2)]
            outs[(c, b)] = o[0] + o[1]
        conv_some(3 * per_phase, per_phase)
        yield

        gather = lambda d: jnp.concatenate(
            [jnp.concatenate([d[(c, b)] for b in range(n_blocks)], axis=0) for c in range(n_cols)], axis=1)
        attn_out = gather(outs).astype(BF16)
        acc = gather(conv)
        mu = jnp.mean(acc, axis=-1, keepdims=True)
        xc = acc - mu
        y = xc * lax.rsqrt(jnp.mean(xc * xc, axis=-1, keepdims=True) + EPS) * lng_ref[...] + lnb_ref[...]
        conv_out = (y * _sigmoid(y)).astype(BF16)
        mixed = jnp.concatenate([attn_out, conv_out], axis=1)
        o_ref[pl.ds(lo, ts), :] = proj_out["x"] + jnp.dot(mixed, wout_ref[...], preferred_element_type=F32)

    proj_out = {}
    for _ in project(0, proj_out):
        pass
    for sub in range(n_sub):
        next_out = {}
        stages = [mix(sub, proj_out)]
        if sub + 1 < n_sub:
            stages.append(project(sub + 1, next_out))
        while stages:
            for g in list(stages):
                if next(g, StopIteration) is StopIteration:
                    stages.remove(g)
        proj_out = next_out

    kv_ref[:, 0:BLOCK, :] = kv_ref[:, tm:tm + BLOCK, :]
    hg_ref[0:CONV_HIST, :] = hg_ref[tm:tm + CONV_HIST, :]


def _mixer(x, layer, cos, sin, norm_g, w_in, conv_w, conv_b, ln_g, ln_b, sinks, w_out, *, tm=1024, ts=256):
    B, S, d = x.shape
    d_in = w_in.shape[2]
    conv_c = conv_w.shape[2]
    d_mix = w_out.shape[1]
    assert conv_c == ATTN_WIDTH
    tile = lambda width: pl.BlockSpec((None, tm, width), lambda b, s: (b, s, 0))
    return pl.pallas_call(
        functools.partial(_mixer_kernel, tm=tm, ts=ts, layer=layer),
        grid=(B, S // tm),
        in_specs=[pl.BlockSpec(memory_space=pltpu.SMEM),
                  tile(d), tile(LANES), tile(LANES),
                  _layer_spec(layer, 1, d),
                  _layer_spec(0, d, d_in),
                  _layer_spec(layer, CONV_WIDTH, conv_c),
                  _layer_spec(layer, 1, conv_c),
                  _layer_spec(layer, 1, conv_c),
                  _layer_spec(layer, 1, conv_c),
                  _layer_spec(0, d_mix, d)],
        out_specs=tile(d),
        out_shape=jax.ShapeDtypeStruct((B, S, d), F32),
        scratch_shapes=[pltpu.VMEM((8, BLOCK + tm, LANES), BF16),
                        pltpu.VMEM((CONV_HIST + tm, conv_c), F32),
                        pltpu.VMEM((2, SUBLANES - 1, SUBLANES + CONV_HIST + ts, conv_c), F32)],
        compiler_params=pltpu.CompilerParams(
            dimension_semantics=("arbitrary", "arbitrary"), vmem_limit_bytes=VMEM_LIMIT_BYTES),
        name="mixer",
    )(sinks, x, cos, sin, norm_g, w_in, conv_w, conv_b, ln_g, ln_b, w_out)


def kernel(x, positions, ffn1_norm, ffn1_w_gate, ffn1_w_up, ffn1_w_down, mix_norm, w_in, conv_w, conv_b,
           conv_ln_g, conv_ln_b, attn_sinks, w_out, ffn2_norm, ffn2_w_gate, ffn2_w_up, ffn2_w_down,
           final_norm):
    B, S, d = x.shape
    depth = w_in.shape[0]
    row = lambda p: p.reshape(depth, 1, p.shape[-1])
    ffn1_f32 = (ffn1_w_gate, ffn1_w_up, ffn1_w_down)
    ffn2_f32 = (ffn2_w_gate, ffn2_w_up, ffn2_w_down)
    ffn1_g, ffn2_g = row(ffn1_norm), row(ffn2_norm)
    final_g = final_norm.reshape(1, d)
    cos, sin, weights = _rope_tables(positions, (ffn1_f32, 0))
    x = x.reshape(B * S, d)
    for l in range(depth):
        last = l == depth - 1
        x, cast = _ffn(x, 0, ffn1_g[l:l + 1], weights, final_g, final_norm=False,
                       cast_next=(ffn2_f32 + (w_in, w_out), l))
        weights, (w_in_l, w_out_l) = cast[:3], cast[3:]
        x = _mixer(x.reshape(B, S, d), l, cos, sin, row(mix_norm), w_in_l, conv_w, row(conv_b), row(conv_ln_g),
                   row(conv_ln_b), attn_sinks, w_out_l).reshape(B * S, d)
        x, weights = _ffn(x, 0, ffn2_g[l:l + 1], weights, final_g, final_norm=last,
                          cast_next=None if last else (ffn1_f32, l + 1))
    return x.reshape(B, S, d)
```

```python
import functools

import jax
import jax.numpy as jnp
from jax import lax
from jax.experimental import pallas as pl
from jax.experimental.pallas import tpu as pltpu

HEAD_DIM = 64
N_HEADS = 8
N_KV_HEADS = 2
ATTN_WIDTH = N_HEADS * HEAD_DIM
KV_WIDTH = N_KV_HEADS * HEAD_DIM
CONV_WIDTH = 31
BLOCK = 128
ROPE_THETA = 10000.0
EPS = 1e-5
LOG2E = 1.4426950408889634

LANES = 128
SUBLANES = 8
PACK = 16
MXU_COLS = 256
VMEM_LIMIT_BYTES = 56 * 1024 * 1024

CONV_HIST = 32
HALF = HEAD_DIM // 2

BF16 = jnp.bfloat16
F32 = jnp.float32


def _sigmoid(x):
    return 1.0 / (1.0 + jnp.exp(-x))


def _rms_norm(x, g):
    return x * lax.rsqrt(jnp.mean(x * x, axis=-1, keepdims=True) + EPS) * g


def _layer_spec(layer, *shape):
    zeros = (0,) * len(shape)
    return pl.BlockSpec((None,) + shape, lambda *_: (layer,) + zeros, pipeline_mode=pl.Buffered(1))


def _cast_specs(cast_next, steps):
    arrays, layer = cast_next
    in_specs, out_specs, shapes = [], [], []
    for w in arrays:
        slab = w.shape[1] // steps
        assert slab * steps == w.shape[1] and slab % PACK == 0
        in_specs.append(pl.BlockSpec((None, slab, w.shape[2]), lambda i: (layer, i, 0)))
        out_specs.append(pl.BlockSpec((None, slab, w.shape[2]), lambda i: (0, i, 0)))
        shapes.append(jax.ShapeDtypeStruct((1,) + w.shape[1:], BF16))
    return in_specs, out_specs, shapes


def _rope_table_kernel(pos_ref, invf_ref, *refs):
    n_cast = (len(refs) - 2) // 2
    cos_ref, sin_ref = refs[n_cast:n_cast + 2]
    for src, dst in zip(refs[:n_cast], refs[n_cast + 2:]):
        dst[...] = src[...].astype(BF16)
    ang = pos_ref[...].astype(F32) * invf_ref[...]
    lane = lax.broadcasted_iota(jnp.int32, (1, LANES), 1)
    sign = jnp.where((lane & (HEAD_DIM - 1)) < HALF, -1.0, 1.0).astype(F32)
    for table, out_ref, scale in ((jnp.cos(ang), cos_ref, None), (jnp.sin(ang), sin_ref, sign)):
        for i in range(LANES // HALF):
            t = jnp.where((lane >= i * HALF) & (lane < (i + 1) * HALF), table, 0.0)
            t = t + pltpu.roll(t, 2 * HALF, 1)
            t = t + pltpu.roll(t, HALF, 1)
            out_ref[i] = t if scale is None else t * scale


def _rope_tables(positions, cast_next):
    B, S = positions.shape
    n = B * S
    groups = LANES // HALF
    rows = n // groups
    inv_freq = 1.0 / (ROPE_THETA ** (jnp.arange(0, HEAD_DIM, 2, dtype=F32) / HEAD_DIM))
    invf = jnp.tile(inv_freq, groups).reshape(1, LANES)
    pos = jnp.repeat(positions.reshape(groups, rows).T, HALF, axis=1)
    tr = 256
    spec_out = pl.BlockSpec((groups, tr, LANES), lambda i: (0, i, 0))
    cast_in_specs, cast_out_specs, cast_shapes = _cast_specs(cast_next, rows // tr)
    cos, sin, *cast = pl.pallas_call(
        _rope_table_kernel,
        grid=(rows // tr,),
        in_specs=[pl.BlockSpec((tr, LANES), lambda i: (i, 0)),
                  pl.BlockSpec((1, LANES), lambda i: (0, 0))] + cast_in_specs,
        out_specs=[spec_out, spec_out] + cast_out_specs,
        out_shape=[jax.ShapeDtypeStruct((groups, rows, LANES), F32)] * 2 + cast_shapes,
        compiler_params=pltpu.CompilerParams(dimension_semantics=("arbitrary",),
                                             vmem_limit_bytes=VMEM_LIMIT_BYTES),
        name="rope_tables",
    )(pos, invf, *cast_next[0])
    return cos.reshape(B, S, LANES), sin.reshape(B, S, LANES), tuple(cast)


def _ffn_kernel(*refs, chunks, final_norm, parts, n_cast):
    x_ref, g_ref, wg_ref, wu_ref, wd_ref, fg_ref = refs[:6]
    cast_in = refs[6:6 + n_cast]
    o_ref = refs[6 + n_cast]
    cast_out = refs[7 + n_cast:7 + 2 * n_cast]
    a_ref = refs[7 + 2 * n_cast]
    tm = x_ref.shape[0]
    rows = tm // parts

    def hidden(h, lo, width):
        gate = jnp.dot(h, wg_ref[:, lo:lo + width], preferred_element_type=F32)
        up = jnp.dot(h, wu_ref[:, lo:lo + width], preferred_element_type=F32)
        return (gate * _sigmoid(gate) * up).astype(BF16)

    h_parts = []
    lo, width = chunks[0]
    for j in range(parts):
        part = pl.ds(j * rows, rows)
        h_parts.append(_rms_norm(x_ref[part, :], g_ref[...]).astype(BF16))
        a_ref[part, lo:lo + width] = hidden(h_parts[j], lo, width)
    h = jnp.concatenate(h_parts, axis=0)
    for i, (lo, width) in enumerate(chunks[1:]):
        a_ref[:, lo:lo + width] = hidden(h, lo, width)
        if i < n_cast:
            cast_out[i][...] = cast_in[i][...].astype(BF16)
    y = x_ref[...] + 0.5 * jnp.dot(a_ref[...], wd_ref[...], preferred_element_type=F32)
    if final_norm:
        y = _rms_norm(y, fg_ref[...])
    o_ref[...] = y


def _ff_chunks(d_ff, chunk):
    out, lo = [], 0
    while lo < d_ff:
        w = min(chunk, d_ff - lo)
        out.append((lo, w))
        lo += w
    return tuple(out)


def _ffn(x, layer, norm_g, weights, final_g, *, final_norm, cast_next=None, tm=1024, ff_chunk=256, parts=4):
    n, d = x.shape
    w_gate, w_up, w_down = weights
    d_ff = w_gate.shape[2]
    steps = n // tm
    cast_arrays = cast_next[0] if cast_next is not None else ()
    cast_in_specs, cast_out_specs, cast_shapes = _cast_specs(cast_next, steps) if cast_arrays else ([], [], [])
    assert len(cast_arrays) < len(_ff_chunks(d_ff, ff_chunk))
    outs = pl.pallas_call(
        functools.partial(_ffn_kernel, chunks=_ff_chunks(d_ff, ff_chunk), final_norm=final_norm, parts=parts,
                          n_cast=len(cast_arrays)),
        grid=(steps,),
        in_specs=[pl.BlockSpec((tm, d), lambda i: (i, 0)),
                  _layer_spec(layer, 1, d),
                  _layer_spec(layer, d, d_ff),
                  _layer_spec(layer, d, d_ff),
                  _layer_spec(layer, d_ff, d),
                  pl.BlockSpec((1, d), lambda i: (0, 0), pipeline_mode=pl.Buffered(1))] + cast_in_specs,
        out_specs=[pl.BlockSpec((tm, d), lambda i: (i, 0))] + cast_out_specs,
        out_shape=[jax.ShapeDtypeStruct((n, d), F32)] + cast_shapes,
        scratch_shapes=[pltpu.VMEM((tm, d_ff), BF16)],
        compiler_params=pltpu.CompilerParams(
            dimension_semantics=("arbitrary",), vmem_limit_bytes=VMEM_LIMIT_BYTES),
        name="ffn",
    )(x, norm_g, w_gate, w_up, w_down, final_g, *cast_arrays)
    return outs[0], tuple(outs[1:])


def _mixer_kernel(sink_ref, x_ref, cos_ref, sin_ref, g_ref, win_ref, cw_ref, cb_ref, lng_ref, lnb_ref,
                  wout_ref, o_ref, kv_ref, hg_ref, sh_ref, *, tm, ts, layer):
    s_idx = pl.program_id(1)
    conv_c = cw_ref.shape[1]
    n_sub = tm // ts
    n_blocks = ts // BLOCK
    q_end, k_end, v_end = ATTN_WIDTH, ATTN_WIDTH + KV_WIDTH, ATTN_WIDTH + 2 * KV_WIDTH

    @pl.when(s_idx == 0)
    def _():
        kv_ref[:, 0:BLOCK, :] = jnp.zeros((8, BLOCK, LANES), BF16)
        hg_ref[0:CONV_HIST, :] = jnp.zeros((CONV_HIST, conv_c), F32)

    lane = lax.broadcasted_iota(jnp.int32, (1, LANES), 1)
    low_half = (lane & (HEAD_DIM - 1)) < HALF
    first = lane < HEAD_DIM
    zero = jnp.zeros((), F32)

    qi = lax.broadcasted_iota(jnp.int32, (BLOCK, 2 * BLOCK), 0)
    kj = lax.broadcasted_iota(jnp.int32, (BLOCK, 2 * BLOCK), 1)
    in_window = (kj > qi) & (kj <= qi + BLOCK)
    first_block_mask = in_window & ((kj >= BLOCK) | (s_idx > 0))
    neg = jnp.finfo(F32).min
    nt = (((1,), (1,)), ((), ()))

    def project(sub, out):
        lo = sub * ts
        tile_rows = pl.ds(lo, ts)
        x = x_ref[tile_rows, :]
        out["x"] = x
        h = _rms_norm(x, g_ref[...]).astype(BF16)
        cos = cos_ref[tile_rows, :]
        sin = sin_ref[tile_rows, :]

        def proj(lo_col, width):
            return jnp.dot(h, win_ref[:, lo_col:lo_col + width], preferred_element_type=F32)

        def rope(t, cos, sin):
            partner = jnp.where(low_half, pltpu.roll(t, LANES - HALF, 1), pltpu.roll(t, HALF, 1))
            return t * cos + partner * sin

        cos_q, sin_q = cos * (HEAD_DIM ** -0.5 * LOG2E), sin * (HEAD_DIM ** -0.5 * LOG2E)
        out["q"] = []
        for lo_col in range(0, q_end, MXU_COLS):
            pq = proj(lo_col, MXU_COLS)
            out["q"] += [rope(pq[:, i * LANES:(i + 1) * LANES], cos_q, sin_q).astype(BF16)
                         for i in range(MXU_COLS // LANES)]
            yield

        pkv = proj(q_end, 2 * KV_WIDTH)
        k = rope(pkv[:, :KV_WIDTH], cos, sin)
        v = pkv[:, KV_WIDTH:]
        rows = pl.ds(BLOCK + lo, ts)
        k_sw = pltpu.roll(k, HEAD_DIM, 1)
        v_sw = pltpu.roll(v, HEAD_DIM, 1)
        for i, t in enumerate((jnp.where(first, k, zero), jnp.where(first, zero, k_sw),
                               jnp.where(first, k_sw, zero), jnp.where(first, zero, k),
                               jnp.where(first, v, zero), jnp.where(first, zero, v_sw),
                               jnp.where(first, v_sw, zero), jnp.where(first, zero, v))):
            kv_ref[i, rows, :] = t.astype(BF16)
        yield

        n_rows = CONV_HIST + ts
        for lo_col in range(0, conv_c, MXU_COLS):
            a = proj(v_end + lo_col, MXU_COLS)
            yield
            gate = proj(v_end + conv_c + lo_col, MXU_COLS)
            cols = pl.ds(lo_col, MXU_COLS)
            hg_ref[pl.ds(CONV_HIST + lo, ts), cols] = a * _sigmoid(gate)
            hbuf = hg_ref[pl.ds(lo, n_rows), cols]
            for r in range(1, SUBLANES):
                sh_ref[sub % 2, r - 1, pl.ds(SUBLANES - r, n_rows), cols] = hbuf
            yield

    def conv_piece(sub, b, c):
        lo = sub * ts
        cols = pl.ds(c * LANES, LANES)
        acc = jnp.broadcast_to(cb_ref[:, cols], (BLOCK, LANES))
        for w in range(CONV_WIDTH):
            base, r = divmod(CONV_HIST - (CONV_WIDTH - 1) + w, SUBLANES)
            if r == 0:
                slab = hg_ref[pl.ds(lo + base * SUBLANES + b * BLOCK, BLOCK), cols]
            else:
                slab = sh_ref[sub % 2, r - 1, pl.ds((base + 1) * SUBLANES + b * BLOCK, BLOCK), cols]
            acc = acc + slab * cw_ref[w:w + 1, cols]
        return acc

    def mix(sub, proj_out):
        lo = sub * ts
        q = proj_out["q"]
        n_cols = ATTN_WIDTH // LANES
        units = [(c, b) for c in range(n_cols) for b in range(n_blocks)]
        conv = {}

        def conv_some(first_unit, count):
            for c, b in units[first_unit:first_unit + count]:
                conv[(c, b)] = conv_piece(sub, b, c)

        per_phase = -(-len(units) // 4)
        heads = [(c, b, e) for c, b in units for e in range(2)]
        kv_of = lambda c: c // (N_HEADS // N_KV_HEADS // 2)
        band_of = lambda b: pl.ds(lo + b * BLOCK, 2 * BLOCK)

        scores = {}
        for c, b, e in heads:
            qb = q[c][b * BLOCK:(b + 1) * BLOCK]
            scores[(c, b, e)] = lax.dot_general(qb, kv_ref[2 * kv_of(c) + e, band_of(b), :], nt,
                                                preferred_element_type=F32)
        conv_some(0, per_phase)
        yield

        row_max = {}
        for c, b, e in heads:
            mask = first_block_mask if lo + b == 0 else in_window
            scores[(c, b, e)] = jnp.where(mask, scores[(c, b, e)], neg)
            row_max[(c, b, e)] = jnp.maximum(jnp.max(scores[(c, b, e)], axis=-1, keepdims=True),
                                             sink_ref[layer, 2 * c + e] * LOG2E)
        conv_some(per_phase, per_phase)
        yield

        probs, recip = {}, {}
        for c, b, e in heads:
            m = row_max[(c, b, e)]
            ex = jnp.exp2(scores[(c, b, e)] - m)
            denom = jnp.sum(ex, axis=-1, keepdims=True) + jnp.exp2(sink_ref[layer, 2 * c + e] * LOG2E - m)
            probs[(c, b, e)] = ex.astype(BF16)
            recip[(c, b, e)] = 1.0 / denom
        conv_some(2 * per_phase, per_phase)
        yield

        outs = {}
        for c, b in units:
            o = [jnp.dot(probs[(c, b, e)], kv_ref[4 + 2 * kv_of(c) + e, band_of(b), :],
                         preferred_element_type=F32) * recip[(c, b, e)] for e in range(2)]
            outs[(c, b)] = o[0] + o[1]
        conv_some(3 * per_phase, per_phase)
        yield

        gather = lambda d: jnp.concatenate(
            [jnp.concatenate([d[(c, b)] for b in range(n_blocks)], axis=0) for c in range(n_cols)], axis=1)
        attn_out = gather(outs).astype(BF16)
        acc = gather(conv)
        mu = jnp.mean(acc, axis=-1, keepdims=True)
        xc = acc - mu
        y = xc * lax.rsqrt(jnp.mean(xc * xc, axis=-1, keepdims=True) + EPS) * lng_ref[...] + lnb_ref[...]
        conv_out = (y * _sigmoid(y)).astype(BF16)
        mixed = jnp.concatenate([attn_out, conv_out], axis=1)
        o_ref[pl.ds(lo, ts), :] = proj_out["x"] + jnp.dot(mixed, wout_ref[...], preferred_element_type=F32)

    proj_out = {}
    for _ in project(0, proj_out):
        pass
    for sub in range(n_sub):
        next_out = {}
        stages = [mix(sub, proj_out)]
        if sub + 1 < n_sub:
            stages.append(project(sub + 1, next_out))
        while stages:
            for g in list(stages):
                if next(g, StopIteration) is StopIteration:
                    stages.remove(g)
        proj_out = next_out

    kv_ref[:, 0:BLOCK, :] = kv_ref[:, tm:tm + BLOCK, :]
    hg_ref[0:CONV_HIST, :] = hg_ref[tm:tm + CONV_HIST, :]


def _mixer(x, layer, cos, sin, norm_g, w_in, conv_w, conv_b, ln_g, ln_b, sinks, w_out, *, tm=1024, ts=256):
    B, S, d = x.shape
    d_in = w_in.shape[2]
    conv_c = conv_w.shape[2]
    d_mix = w_out.shape[1]
    assert conv_c == ATTN_WIDTH
    tile = lambda width: pl.BlockSpec((None, tm, width), lambda b, s: (b, s, 0))
    return pl.pallas_call(
        functools.partial(_mixer_kernel, tm=tm, ts=ts, layer=layer),
        grid=(B, S // tm),
        in_specs=[pl.BlockSpec(memory_space=pltpu.SMEM),
                  tile(d), tile(LANES), tile(LANES),
                  _layer_spec(layer, 1, d),
                  _layer_spec(0, d, d_in),
                  _layer_spec(layer, CONV_WIDTH, conv_c),
                  _layer_spec(layer, 1, conv_c),
                  _layer_spec(layer, 1, conv_c),
                  _layer_spec(layer, 1, conv_c),
                  _layer_spec(0, d_mix, d)],
        out_specs=tile(d),
        out_shape=jax.ShapeDtypeStruct((B, S, d), F32),
        scratch_shapes=[pltpu.VMEM((8, BLOCK + tm, LANES), BF16),
                        pltpu.VMEM((CONV_HIST + tm, conv_c), F32),
                        pltpu.VMEM((2, SUBLANES - 1, SUBLANES + CONV_HIST + ts, conv_c), F32)],
        compiler_params=pltpu.CompilerParams(
            dimension_semantics=("arbitrary", "arbitrary"), vmem_limit_bytes=VMEM_LIMIT_BYTES),
        name="mixer",
    )(sinks, x, cos, sin, norm_g, w_in, conv_w, conv_b, ln_g, ln_b, w_out)


def kernel(x, positions, ffn1_norm, ffn1_w_gate, ffn1_w_up, ffn1_w_down, mix_norm, w_in, conv_w, conv_b,
           conv_ln_g, conv_ln_b, attn_sinks, w_out, ffn2_norm, ffn2_w_gate, ffn2_w_up, ffn2_w_down,
           final_norm):
    B, S, d = x.shape
    depth = w_in.shape[0]
    row = lambda p: p.reshape(depth, 1, p.shape[-1])
    ffn1_f32 = (ffn1_w_gate, ffn1_w_up, ffn1_w_down)
    ffn2_f32 = (ffn2_w_gate, ffn2_w_up, ffn2_w_down)
    ffn1_g, ffn2_g = row(ffn1_norm), row(ffn2_norm)
    final_g = final_norm.reshape(1, d)
    cos, sin, weights = _rope_tables(positions, (ffn1_f32, 0))
    x = x.reshape(B * S, d)
    for l in range(depth):
        last = l == depth - 1
        x, cast = _ffn(x, 0, ffn1_g[l:l + 1], weights, final_g, final_norm=False,
                       cast_next=(ffn2_f32 + (w_in, w_out), l))
        weights, (w_in_l, w_out_l) = cast[:3], cast[3:]
        x = _mixer(x.reshape(B, S, d), l, cos, sin, row(mix_norm), w_in_l, conv_w, row(conv_b), row(conv_ln_g),
                   row(conv_ln_b), attn_sinks, w_out_l).reshape(B * S, d)
        x, weights = _ffn(x, 0, ffn2_g[l:l + 1], weights, final_g, final_norm=last,
                          cast_next=None if last else (ffn1_f32, l + 1))
    return x.reshape(B, S, d)
```

```python
import functools

import jax
import jax.numpy as jnp
from jax import lax
from jax.experimental import pallas as pl
from jax.experimental.pallas import tpu as pltpu

HEAD_DIM = 64
N_HEADS = 8
N_KV_HEADS = 2
ATTN_WIDTH = N_HEADS * HEAD_DIM
KV_WIDTH = N_KV_HEADS * HEAD_DIM
CONV_WIDTH = 31
BLOCK = 128
ROPE_THETA = 10000.0
EPS = 1e-5
LOG2E = 1.4426950408889634

LANES = 128
SUBLANES = 8
PACK = 16
MXU_COLS = 256
VMEM_LIMIT_BYTES = 56 * 1024 * 1024

CONV_HIST = 32
HALF = HEAD_DIM // 2

BF16 = jnp.bfloat16
F32 = jnp.float32


def _sigmoid(x):
    return 1.0 / (1.0 + jnp.exp(-x))


def _rms_norm(x, g):
    return x * lax.rsqrt(jnp.mean(x * x, axis=-1, keepdims=True) + EPS) * g


def _layer_spec(layer, *shape):
    zeros = (0,) * len(shape)
    return pl.BlockSpec((None,) + shape, lambda *_: (layer,) + zeros, pipeline_mode=pl.Buffered(1))


def _cast_specs(cast_next, steps):
    arrays, layer = cast_next
    in_specs, out_specs, shapes = [], [], []
    for w in arrays:
        slab = w.shape[1] // steps
        assert slab * steps == w.shape[1] and slab % PACK == 0
        in_specs.append(pl.BlockSpec((None, slab, w.shape[2]), lambda i: (layer, i, 0)))
        out_specs.append(pl.BlockSpec((None, slab, w.shape[2]), lambda i: (0, i, 0)))
        shapes.append(jax.ShapeDtypeStruct((1,) + w.shape[1:], BF16))
    return in_specs, out_specs, shapes


def _rope_table_kernel(pos_ref, invf_ref, *refs):
    n_cast = (len(refs) - 2) // 2
    cos_ref, sin_ref = refs[n_cast:n_cast + 2]
    for src, dst in zip(refs[:n_cast], refs[n_cast + 2:]):
        dst[...] = src[...].astype(BF16)
    ang = pos_ref[...].astype(F32) * invf_ref[...]
    lane = lax.broadcasted_iota(jnp.int32, (1, LANES), 1)
    sign = jnp.where((lane & (HEAD_DIM - 1)) < HALF, -1.0, 1.0).astype(F32)
    for table, out_ref, scale in ((jnp.cos(ang), cos_ref, None), (jnp.sin(ang), sin_ref, sign)):
        for i in range(LANES // HALF):
            t = jnp.where((lane >= i * HALF) & (lane < (i + 1) * HALF), table, 0.0)
            t = t + pltpu.roll(t, 2 * HALF, 1)
            t = t + pltpu.roll(t, HALF, 1)
            out_ref[i] = t if scale is None else t * scale


def _rope_tables(positions, cast_next):
    B, S = positions.shape
    n = B * S
    groups = LANES // HALF
    rows = n // groups
    inv_freq = 1.0 / (ROPE_THETA ** (jnp.arange(0, HEAD_DIM, 2, dtype=F32) / HEAD_DIM))
    invf = jnp.tile(inv_freq, groups).reshape(1, LANES)
    pos = jnp.repeat(positions.reshape(groups, rows).T, HALF, axis=1)
    tr = 1024
    spec_out = pl.BlockSpec((groups, tr, LANES), lambda i: (0, i, 0))
    cast_in_specs, cast_out_specs, cast_shapes = _cast_specs(cast_next, rows // tr)
    cos, sin, *cast = pl.pallas_call(
        _rope_table_kernel,
        grid=(rows // tr,),
        in_specs=[pl.BlockSpec((tr, LANES), lambda i: (i, 0)),
                  pl.BlockSpec((1, LANES), lambda i: (0, 0))] + cast_in_specs,
        out_specs=[spec_out, spec_out] + cast_out_specs,
        out_shape=[jax.ShapeDtypeStruct((groups, rows, LANES), F32)] * 2 + cast_shapes,
        compiler_params=pltpu.CompilerParams(dimension_semantics=("arbitrary",),
                                             vmem_limit_bytes=VMEM_LIMIT_BYTES),
        name="rope_tables",
    )(pos, invf, *cast_next[0])
    return cos.reshape(B, S, LANES), sin.reshape(B, S, LANES), tuple(cast)


def _ffn_kernel(*refs, chunks, final_norm, parts, n_cast):
    x_ref, g_ref, wg_ref, wu_ref, wd_ref, fg_ref = refs[:6]
    cast_in = refs[6:6 + n_cast]
    o_ref = refs[6 + n_cast]
    cast_out = refs[7 + n_cast:7 + 2 * n_cast]
    a_ref = refs[7 + 2 * n_cast]
    tm = x_ref.shape[0]
    rows = tm // parts

    def hidden(h, lo, width):
        gate = jnp.dot(h, wg_ref[:, lo:lo + width], preferred_element_type=F32)
        up = jnp.dot(h, wu_ref[:, lo:lo + width], preferred_element_type=F32)
        return (gate * _sigmoid(gate) * up).astype(BF16)

    h_parts = []
    lo, width = chunks[0]
    for j in range(parts):
        part = pl.ds(j * rows, rows)
        h_parts.append(_rms_norm(x_ref[part, :], g_ref[...]).astype(BF16))
        a_ref[part, lo:lo + width] = hidden(h_parts[j], lo, width)
    h = jnp.concatenate(h_parts, axis=0)
    for i, (lo, width) in enumerate(chunks[1:]):
        a_ref[:, lo:lo + width] = hidden(h, lo, width)
        if i < n_cast:
            cast_out[i][...] = cast_in[i][...].astype(BF16)
    y = x_ref[...] + 0.5 * jnp.dot(a_ref[...], wd_ref[...], preferred_element_type=F32)
    if final_norm:
        y = _rms_norm(y, fg_ref[...])
    o_ref[...] = y


def _ff_chunks(d_ff, chunk):
    out, lo = [], 0
    while lo < d_ff:
        w = min(chunk, d_ff - lo)
        out.append((lo, w))
        lo += w
    return tuple(out)


def _ffn(x, layer, norm_g, weights, final_g, *, final_norm, cast_next=None, tm=1024, ff_chunk=256, parts=4):
    n, d = x.shape
    w_gate, w_up, w_down = weights
    d_ff = w_gate.shape[2]
    steps = n // tm
    cast_arrays = cast_next[0] if cast_next is not None else ()
    cast_in_specs, cast_out_specs, cast_shapes = _cast_specs(cast_next, steps) if cast_arrays else ([], [], [])
    assert len(cast_arrays) < len(_ff_chunks(d_ff, ff_chunk))
    outs = pl.pallas_call(
        functools.partial(_ffn_kernel, chunks=_ff_chunks(d_ff, ff_chunk), final_norm=final_norm, parts=parts,
                          n_cast=len(cast_arrays)),
        grid=(steps,),
        in_specs=[pl.BlockSpec((tm, d), lambda i: (i, 0)),
                  _layer_spec(layer, 1, d),
                  _layer_spec(layer, d, d_ff),
                  _layer_spec(layer, d, d_ff),
                  _layer_spec(layer, d_ff, d),
                  pl.BlockSpec((1, d), lambda i: (0, 0), pipeline_mode=pl.Buffered(1))] + cast_in_specs,
        out_specs=[pl.BlockSpec((tm, d), lambda i: (i, 0))] + cast_out_specs,
        out_shape=[jax.ShapeDtypeStruct((n, d), F32)] + cast_shapes,
        scratch_shapes=[pltpu.VMEM((tm, d_ff), BF16)],
        compiler_params=pltpu.CompilerParams(
            dimension_semantics=("arbitrary",), vmem_limit_bytes=VMEM_LIMIT_BYTES),
        name="ffn",
    )(x, norm_g, w_gate, w_up, w_down, final_g, *cast_arrays)
    return outs[0], tuple(outs[1:])


def _mixer_kernel(sink_ref, x_ref, cos_ref, sin_ref, g_ref, win_ref, cw_ref, cb_ref, lng_ref, lnb_ref,
                  wout_ref, o_ref, kv_ref, hg_ref, sh_ref, *, tm, ts, layer):
    s_idx = pl.program_id(1)
    conv_c = cw_ref.shape[1]
    n_sub = tm // ts
    n_blocks = ts // BLOCK
    q_end, k_end, v_end = ATTN_WIDTH, ATTN_WIDTH + KV_WIDTH, ATTN_WIDTH + 2 * KV_WIDTH

    @pl.when(s_idx == 0)
    def _():
        kv_ref[:, 0:BLOCK, :] = jnp.zeros((8, BLOCK, LANES), BF16)
        hg_ref[0:CONV_HIST, :] = jnp.zeros((CONV_HIST, conv_c), F32)

    lane = lax.broadcasted_iota(jnp.int32, (1, LANES), 1)
    low_half = (lane & (HEAD_DIM - 1)) < HALF
    first = lane < HEAD_DIM
    zero = jnp.zeros((), F32)

    qi = lax.broadcasted_iota(jnp.int32, (BLOCK, 2 * BLOCK), 0)
    kj = lax.broadcasted_iota(jnp.int32, (BLOCK, 2 * BLOCK), 1)
    in_window = (kj > qi) & (kj <= qi + BLOCK)
    first_block_mask = in_window & ((kj >= BLOCK) | (s_idx > 0))
    neg = jnp.finfo(F32).min
    nt = (((1,), (1,)), ((), ()))

    def project(sub, out):
        lo = sub * ts
        tile_rows = pl.ds(lo, ts)
        x = x_ref[tile_rows, :]
        out["x"] = x
        h = _rms_norm(x, g_ref[...]).astype(BF16)
        cos = cos_ref[tile_rows, :]
        sin = sin_ref[tile_rows, :]

        def proj(lo_col, width):
            return jnp.dot(h, win_ref[:, lo_col:lo_col + width], preferred_element_type=F32)

        def rope(t, cos, sin):
            partner = jnp.where(low_half, pltpu.roll(t, LANES - HALF, 1), pltpu.roll(t, HALF, 1))
            return t * cos + partner * sin

        cos_q, sin_q = cos * (HEAD_DIM ** -0.5 * LOG2E), sin * (HEAD_DIM ** -0.5 * LOG2E)
        out["q"] = []
        for lo_col in range(0, q_end, MXU_COLS):
            pq = proj(lo_col, MXU_COLS)
            out["q"] += [rope(pq[:, i * LANES:(i + 1) * LANES], cos_q, sin_q).astype(BF16)
                         for i in range(MXU_COLS // LANES)]
            yield

        pkv = proj(q_end, 2 * KV_WIDTH)
        k = rope(pkv[:, :KV_WIDTH], cos, sin)
        v = pkv[:, KV_WIDTH:]
        rows = pl.ds(BLOCK + lo, ts)
        k_sw = pltpu.roll(k, HEAD_DIM, 1)
        v_sw = pltpu.roll(v, HEAD_DIM, 1)
        for i, t in enumerate((jnp.where(first, k, zero), jnp.where(first, zero, k_sw),
                               jnp.where(first, k_sw, zero), jnp.where(first, zero, k),
                               jnp.where(first, v, zero), jnp.where(first, zero, v_sw),
                               jnp.where(first, v_sw, zero), jnp.where(first, zero, v))):
            kv_ref[i, rows, :] = t.astype(BF16)
        yield

        n_rows = CONV_HIST + ts
        for lo_col in range(0, conv_c, MXU_COLS):
            a = proj(v_end + lo_col, MXU_COLS)
            yield
            gate = proj(v_end + conv_c + lo_col, MXU_COLS)
            cols = pl.ds(lo_col, MXU_COLS)
            hg_ref[pl.ds(CONV_HIST + lo, ts), cols] = a * _sigmoid(gate)
            hbuf = hg_ref[pl.ds(lo, n_rows), cols]
            for r in range(1, SUBLANES):
                sh_ref[sub % 2, r - 1, pl.ds(SUBLANES - r, n_rows), cols] = hbuf
            yield

    def conv_piece(sub, b, c):
        lo = sub * ts
        cols = pl.ds(c * LANES, LANES)
        acc = jnp.broadcast_to(cb_ref[:, cols], (BLOCK, LANES))
        for w in range(CONV_WIDTH):
            base, r = divmod(CONV_HIST - (CONV_WIDTH - 1) + w, SUBLANES)
            if r == 0:
                slab = hg_ref[pl.ds(lo + base * SUBLANES + b * BLOCK, BLOCK), cols]
            else:
                slab = sh_ref[sub % 2, r - 1, pl.ds((base + 1) * SUBLANES + b * BLOCK, BLOCK), cols]
            acc = acc + slab * cw_ref[w:w + 1, cols]
        return acc

    def mix(sub, proj_out):
        lo = sub * ts
        q = proj_out["q"]
        n_cols = ATTN_WIDTH // LANES
        units = [(c, b) for c in range(n_cols) for b in range(n_blocks)]
        conv = {}

        def conv_some(first_unit, count):
            for c, b in units[first_unit:first_unit + count]:
                conv[(c, b)] = conv_piece(sub, b, c)

        per_phase = -(-len(units) // 4)
        heads = [(c, b, e) for c, b in units for e in range(2)]
        kv_of = lambda c: c // (N_HEADS // N_KV_HEADS // 2)
        band_of = lambda b: pl.ds(lo + b * BLOCK, 2 * BLOCK)

        scores = {}
        for c, b, e in heads:
            qb = q[c][b * BLOCK:(b + 1) * BLOCK]
            scores[(c, b, e)] = lax.dot_general(qb, kv_ref[2 * kv_of(c) + e, band_of(b), :], nt,
                                                preferred_element_type=F32)
        conv_some(0, per_phase)
        yield

        row_max = {}
        for c, b, e in heads:
            mask = first_block_mask if lo + b == 0 else in_window
            scores[(c, b, e)] = jnp.where(mask, scores[(c, b, e)], neg)
            row_max[(c, b, e)] = jnp.maximum(jnp.max(scores[(c, b, e)], axis=-1, keepdims=True),
                                             sink_ref[layer, 2 * c + e] * LOG2E)
        conv_some(per_phase, per_phase)
        yield

        probs, recip = {}, {}
        for c, b, e in heads:
            m = row_max[(c, b, e)]
            ex = jnp.exp2(scores[(c, b, e)] - m)
            denom = jnp.sum(ex, axis=-1, keepdims=True) + jnp.exp2(sink_ref[layer, 2 * c + e] * LOG2E - m)
            probs[(c, b, e)] = ex.astype(BF16)
            recip[(c, b, e)] = 1.0 / denom
        conv_some(2 * per_phase, per_phase)
        yield

        outs = {}
        for c, b in units:
            o = [jnp.dot(probs[(c, b, e)], kv_ref[4 + 2 * kv_of(c) + e, band_of(b), :],
                         preferred_element_type=F32) * recip[(c, b, e)] for e in range(2)]
            outs[(c, b)] = o[0] + o[1]
        conv_some(3 * per_phase, per_phase)
        yield

        gather = lambda d: jnp.concatenate(
            [jnp.concatenate([d[(c, b)] for b in range(n_blocks)], axis=0) for c in range(n_cols)], axis=1)
        attn_out = gather(outs).astype(BF16)
        acc = gather(conv)
        mu = jnp.mean(acc, axis=-1, keepdims=True)
        xc = acc - mu
        y = xc * lax.rsqrt(jnp.mean(xc * xc, axis=-1, keepdims=True) + EPS) * lng_ref[...] + lnb_ref[...]
        conv_out = (y * _sigmoid(y)).astype(BF16)
        mixed = jnp.concatenate([attn_out, conv_out], axis=1)
        o_ref[pl.ds(lo, ts), :] = proj_out["x"] + jnp.dot(mixed, wout_ref[...], preferred_element_type=F32)

    proj_out = {}
    for _ in project(0, proj_out):
        pass
    for sub in range(n_sub):
        next_out = {}
        stages = [mix(sub, proj_out)]
        if sub + 1 < n_sub:
            stages.append(project(sub + 1, next_out))
        while stages:
            for g in list(stages):
                if next(g, StopIteration) is StopIteration:
                    stages.remove(g)
        proj_out = next_out

    kv_ref[:, 0:BLOCK, :] = kv_ref[:, tm:tm + BLOCK, :]
    hg_ref[0:CONV_HIST, :] = hg_ref[tm:tm + CONV_HIST, :]


def _mixer(x, layer, cos, sin, norm_g, w_in, conv_w, conv_b, ln_g, ln_b, sinks, w_out, *, tm=1024, ts=256):
    B, S, d = x.shape
    d_in = w_in.shape[2]
    conv_c = conv_w.shape[2]
    d_mix = w_out.shape[1]
    assert conv_c == ATTN_WIDTH
    tile = lambda width: pl.BlockSpec((None, tm, width), lambda b, s: (b, s, 0))
    return pl.pallas_call(
        functools.partial(_mixer_kernel, tm=tm, ts=ts, layer=layer),
        grid=(B, S // tm),
        in_specs=[pl.BlockSpec(memory_space=pltpu.SMEM),
                  tile(d), tile(LANES), tile(LANES),
                  _layer_spec(layer, 1, d),
                  _layer_spec(0, d, d_in),
                  _layer_spec(layer, CONV_WIDTH, conv_c),
                  _layer_spec(layer, 1, conv_c),
                  _layer_spec(layer, 1, conv_c),
                  _layer_spec(layer, 1, conv_c),
                  _layer_spec(0, d_mix, d)],
        out_specs=tile(d),
        out_shape=jax.ShapeDtypeStruct((B, S, d), F32),
        scratch_shapes=[pltpu.VMEM((8, BLOCK + tm, LANES), BF16),
                        pltpu.VMEM((CONV_HIST + tm, conv_c), F32),
                        pltpu.VMEM((2, SUBLANES - 1, SUBLANES + CONV_HIST + ts, conv_c), F32)],
        compiler_params=pltpu.CompilerParams(
            dimension_semantics=("arbitrary", "arbitrary"), vmem_limit_bytes=VMEM_LIMIT_BYTES),
        name="mixer",
    )(sinks, x, cos, sin, norm_g, w_in, conv_w, conv_b, ln_g, ln_b, w_out)


def kernel(x, positions, ffn1_norm, ffn1_w_gate, ffn1_w_up, ffn1_w_down, mix_norm, w_in, conv_w, conv_b,
           conv_ln_g, conv_ln_b, attn_sinks, w_out, ffn2_norm, ffn2_w_gate, ffn2_w_up, ffn2_w_down,
           final_norm):
    B, S, d = x.shape
    depth = w_in.shape[0]
    row = lambda p: p.reshape(depth, 1, p.shape[-1])
    ffn1_f32 = (ffn1_w_gate, ffn1_w_up, ffn1_w_down)
    ffn2_f32 = (ffn2_w_gate, ffn2_w_up, ffn2_w_down)
    ffn1_g, ffn2_g = row(ffn1_norm), row(ffn2_norm)
    final_g = final_norm.reshape(1, d)
    cos, sin, weights = _rope_tables(positions, (ffn1_f32, 0))
    x = x.reshape(B * S, d)
    for l in range(depth):
        last = l == depth - 1
        x, cast = _ffn(x, 0, ffn1_g[l:l + 1], weights, final_g, final_norm=False,
                       cast_next=(ffn2_f32 + (w_in, w_out), l))
        weights, (w_in_l, w_out_l) = cast[:3], cast[3:]
        x = _mixer(x.reshape(B, S, d), l, cos, sin, row(mix_norm), w_in_l, conv_w, row(conv_b), row(conv_ln_g),
                   row(conv_ln_b), attn_sinks, w_out_l).reshape(B * S, d)
        x, weights = _ffn(x, 0, ffn2_g[l:l + 1], weights, final_g, final_norm=last,
                          cast_next=None if last else (ffn1_f32, l + 1))
    return x.reshape(B, S, d)
```

```python
import functools

import jax
import jax.numpy as jnp
from jax import lax
from jax.experimental import pallas as pl
from jax.experimental.pallas import tpu as pltpu

HEAD_DIM = 64
N_HEADS = 8
N_KV_HEADS = 2
ATTN_WIDTH = N_HEADS * HEAD_DIM
KV_WIDTH = N_KV_HEADS * HEAD_DIM
CONV_WIDTH = 31
BLOCK = 128
ROPE_THETA = 10000.0
EPS = 1e-5
LOG2E = 1.4426950408889634
MASK_BIAS = -1e30

LANES = 128
SUBLANES = 8
PACK = 16
MXU_COLS = 256
VMEM_LIMIT_BYTES = 56 * 1024 * 1024

CONV_HIST = 32
HALF = HEAD_DIM // 2

BF16 = jnp.bfloat16
F32 = jnp.float32


def _sigmoid(x):
    return 1.0 / (1.0 + jnp.exp(-x))


def _rms_norm(x, g):
    return x * lax.rsqrt(jnp.mean(x * x, axis=-1, keepdims=True) + EPS) * g


def _layer_spec(layer, *shape):
    zeros = (0,) * len(shape)
    return pl.BlockSpec((None,) + shape, lambda *_: (layer,) + zeros, pipeline_mode=pl.Buffered(1))


def _cast_specs(cast_next, steps):
    arrays, layer = cast_next
    in_specs, out_specs, shapes = [], [], []
    for w in arrays:
        slab = w.shape[1] // steps
        assert slab * steps == w.shape[1] and slab % PACK == 0
        in_specs.append(pl.BlockSpec((None, slab, w.shape[2]), lambda i: (layer, i, 0)))
        out_specs.append(pl.BlockSpec((None, slab, w.shape[2]), lambda i: (0, i, 0)))
        shapes.append(jax.ShapeDtypeStruct((1,) + w.shape[1:], BF16))
    return in_specs, out_specs, shapes


def _rope_table_kernel(pos_ref, invf_ref, *refs):
    n_cast = (len(refs) - 2) // 2
    cos_ref, sin_ref = refs[n_cast:n_cast + 2]
    for src, dst in zip(refs[:n_cast], refs[n_cast + 2:]):
        dst[...] = src[...].astype(BF16)
    ang = pos_ref[...].astype(F32) * invf_ref[...]
    lane = lax.broadcasted_iota(jnp.int32, (1, LANES), 1)
    sign = jnp.where((lane & (HEAD_DIM - 1)) < HALF, -1.0, 1.0).astype(F32)
    for table, out_ref, scale in ((jnp.cos(ang), cos_ref, None), (jnp.sin(ang), sin_ref, sign)):
        for i in range(LANES // HALF):
            t = jnp.where((lane >= i * HALF) & (lane < (i + 1) * HALF), table, 0.0)
            t = t + pltpu.roll(t, 2 * HALF, 1)
            t = t + pltpu.roll(t, HALF, 1)
            out_ref[i] = t if scale is None else t * scale


def _rope_tables(positions, cast_next):
    B, S = positions.shape
    n = B * S
    groups = LANES // HALF
    rows = n // groups
    inv_freq = 1.0 / (ROPE_THETA ** (jnp.arange(0, HEAD_DIM, 2, dtype=F32) / HEAD_DIM))
    invf = jnp.tile(inv_freq, groups).reshape(1, LANES)
    pos = jnp.repeat(positions.reshape(groups, rows).T, HALF, axis=1)
    tr = 1024
    spec_out = pl.BlockSpec((groups, tr, LANES), lambda i: (0, i, 0))
    cast_in_specs, cast_out_specs, cast_shapes = _cast_specs(cast_next, rows // tr)
    cos, sin, *cast = pl.pallas_call(
        _rope_table_kernel,
        grid=(rows // tr,),
        in_specs=[pl.BlockSpec((tr, LANES), lambda i: (i, 0)),
                  pl.BlockSpec((1, LANES), lambda i: (0, 0))] + cast_in_specs,
        out_specs=[spec_out, spec_out] + cast_out_specs,
        out_shape=[jax.ShapeDtypeStruct((groups, rows, LANES), F32)] * 2 + cast_shapes,
        compiler_params=pltpu.CompilerParams(dimension_semantics=("arbitrary",),
                                             vmem_limit_bytes=VMEM_LIMIT_BYTES),
        name="rope_tables",
    )(pos, invf, *cast_next[0])
    return cos.reshape(B, S, LANES), sin.reshape(B, S, LANES), tuple(cast)


def _ffn_kernel(*refs, chunks, final_norm, parts, n_cast):
    x_ref, g_ref, wg_ref, wu_ref, wd_ref, fg_ref = refs[:6]
    cast_in = refs[6:6 + n_cast]
    o_ref = refs[6 + n_cast]
    cast_out = refs[7 + n_cast:7 + 2 * n_cast]
    a_ref = refs[7 + 2 * n_cast]
    tm = x_ref.shape[0]
    rows = tm // parts

    def hidden(h, lo, width):
        gate = jnp.dot(h, wg_ref[:, lo:lo + width], preferred_element_type=F32)
        up = jnp.dot(h, wu_ref[:, lo:lo + width], preferred_element_type=F32)
        return (gate * _sigmoid(gate) * up).astype(BF16)

    h_parts = []
    lo, width = chunks[0]
    for j in range(parts):
        part = pl.ds(j * rows, rows)
        h_parts.append(_rms_norm(x_ref[part, :], g_ref[...]).astype(BF16))
        a_ref[part, lo:lo + width] = hidden(h_parts[j], lo, width)
    h = jnp.concatenate(h_parts, axis=0)
    for i, (lo, width) in enumerate(chunks[1:]):
        a_ref[:, lo:lo + width] = hidden(h, lo, width)
        if i < n_cast:
            cast_out[i][...] = cast_in[i][...].astype(BF16)
    y = x_ref[...] + 0.5 * jnp.dot(a_ref[...], wd_ref[...], preferred_element_type=F32)
    if final_norm:
        y = _rms_norm(y, fg_ref[...])
    o_ref[...] = y


def _ff_chunks(d_ff, chunk):
    out, lo = [], 0
    while lo < d_ff:
        w = min(chunk, d_ff - lo)
        out.append((lo, w))
        lo += w
    return tuple(out)


def _ffn(x, layer, norm_g, weights, final_g, *, final_norm, cast_next=None, tm=1024, ff_chunk=256, parts=4):
    n, d = x.shape
    w_gate, w_up, w_down = weights
    d_ff = w_gate.shape[2]
    steps = n // tm
    cast_arrays = cast_next[0] if cast_next is not None else ()
    cast_in_specs, cast_out_specs, cast_shapes = _cast_specs(cast_next, steps) if cast_arrays else ([], [], [])
    assert len(cast_arrays) < len(_ff_chunks(d_ff, ff_chunk))
    outs = pl.pallas_call(
        functools.partial(_ffn_kernel, chunks=_ff_chunks(d_ff, ff_chunk), final_norm=final_norm, parts=parts,
                          n_cast=len(cast_arrays)),
        grid=(steps,),
        in_specs=[pl.BlockSpec((tm, d), lambda i: (i, 0)),
                  _layer_spec(layer, 1, d),
                  _layer_spec(layer, d, d_ff),
                  _layer_spec(layer, d, d_ff),
                  _layer_spec(layer, d_ff, d),
                  pl.BlockSpec((1, d), lambda i: (0, 0), pipeline_mode=pl.Buffered(1))] + cast_in_specs,
        out_specs=[pl.BlockSpec((tm, d), lambda i: (i, 0))] + cast_out_specs,
        out_shape=[jax.ShapeDtypeStruct((n, d), F32)] + cast_shapes,
        scratch_shapes=[pltpu.VMEM((tm, d_ff), BF16)],
        compiler_params=pltpu.CompilerParams(
            dimension_semantics=("arbitrary",), vmem_limit_bytes=VMEM_LIMIT_BYTES),
        name="ffn",
    )(x, norm_g, w_gate, w_up, w_down, final_g, *cast_arrays)
    return outs[0], tuple(outs[1:])


def _mixer_kernel(sink_ref, x_ref, cos_ref, sin_ref, g_ref, win_ref, cw_ref, cb_ref, lng_ref, lnb_ref,
                  wout_ref, o_ref, kv_ref, hg_ref, sh_ref, *, tm, ts, layer):
    s_idx = pl.program_id(1)
    conv_c = cw_ref.shape[1]
    n_sub = tm // ts
    n_blocks = ts // BLOCK
    q_end, k_end, v_end = ATTN_WIDTH, ATTN_WIDTH + KV_WIDTH, ATTN_WIDTH + 2 * KV_WIDTH

    @pl.when(s_idx == 0)
    def _():
        kv_ref[:, 0:BLOCK, :] = jnp.zeros((8, BLOCK, LANES), BF16)
        hg_ref[0:CONV_HIST, :] = jnp.zeros((CONV_HIST, conv_c), F32)

    lane = lax.broadcasted_iota(jnp.int32, (1, LANES), 1)
    low_half = (lane & (HEAD_DIM - 1)) < HALF
    first = lane < HEAD_DIM
    zero = jnp.zeros((), F32)

    kj = lax.broadcasted_iota(jnp.int32, (2 * BLOCK, BLOCK), 0)
    qi = lax.broadcasted_iota(jnp.int32, (2 * BLOCK, BLOCK), 1)
    in_window = (kj > qi) & (kj <= qi + BLOCK)
    first_block = in_window & ((kj >= BLOCK) | (s_idx > 0))
    bias_t = jnp.where(in_window, 0.0, MASK_BIAS).astype(BF16)
    first_bias_t = jnp.where(first_block, 0.0, MASK_BIAS).astype(BF16)
    eye = (lax.broadcasted_iota(jnp.int32, (BLOCK, BLOCK), 0)
           == lax.broadcasted_iota(jnp.int32, (BLOCK, BLOCK), 1)).astype(BF16)
    nt = (((1,), (1,)), ((), ()))

    def project(sub, out):
        lo = sub * ts
        tile_rows = pl.ds(lo, ts)
        x = x_ref[tile_rows, :]
        out["x"] = x
        h = _rms_norm(x, g_ref[...]).astype(BF16)
        cos = cos_ref[tile_rows, :]
        sin = sin_ref[tile_rows, :]

        def proj(lo_col, width):
            return jnp.dot(h, win_ref[:, lo_col:lo_col + width], preferred_element_type=F32)

        def rope(t, cos, sin):
            partner = jnp.where(low_half, pltpu.roll(t, LANES - HALF, 1), pltpu.roll(t, HALF, 1))
            return t * cos + partner * sin

        cos_q, sin_q = cos * (HEAD_DIM ** -0.5 * LOG2E), sin * (HEAD_DIM ** -0.5 * LOG2E)
        out["q"] = []
        for lo_col in range(0, q_end, MXU_COLS):
            pq = proj(lo_col, MXU_COLS)
            out["q"] += [rope(pq[:, i * LANES:(i + 1) * LANES], cos_q, sin_q).astype(BF16)
                         for i in range(MXU_COLS // LANES)]
            yield

        pkv = proj(q_end, 2 * KV_WIDTH)
        k = rope(pkv[:, :KV_WIDTH], cos, sin)
        v = pkv[:, KV_WIDTH:]
        rows = pl.ds(BLOCK + lo, ts)
        k_sw = pltpu.roll(k, HEAD_DIM, 1)
        v_sw = pltpu.roll(v, HEAD_DIM, 1)
        for i, t in enumerate((jnp.where(first, k, zero), jnp.where(first, zero, k_sw),
                               jnp.where(first, k_sw, zero), jnp.where(first, zero, k),
                               jnp.where(first, v, zero), jnp.where(first, zero, v_sw),
                               jnp.where(first, v_sw, zero), jnp.where(first, zero, v))):
            kv_ref[i, rows, :] = t.astype(BF16)
        yield

        n_rows = CONV_HIST + ts
        for lo_col in range(0, conv_c, MXU_COLS):
            a = proj(v_end + lo_col, MXU_COLS)
            yield
            gate = proj(v_end + conv_c + lo_col, MXU_COLS)
            cols = pl.ds(lo_col, MXU_COLS)
            hg_ref[pl.ds(CONV_HIST + lo, ts), cols] = a * _sigmoid(gate)
            hbuf = hg_ref[pl.ds(lo, n_rows), cols]
            for r in range(1, SUBLANES):
                sh_ref[sub % 2, r - 1, pl.ds(SUBLANES - r, n_rows), cols] = hbuf
            yield

    def conv_piece(sub, b, c):
        lo = sub * ts
        cols = pl.ds(c * LANES, LANES)
        acc = jnp.broadcast_to(cb_ref[:, cols], (BLOCK, LANES))
        for w in range(CONV_WIDTH):
            base, r = divmod(CONV_HIST - (CONV_WIDTH - 1) + w, SUBLANES)
            if r == 0:
                slab = hg_ref[pl.ds(lo + base * SUBLANES + b * BLOCK, BLOCK), cols]
            else:
                slab = sh_ref[sub % 2, r - 1, pl.ds((base + 1) * SUBLANES + b * BLOCK, BLOCK), cols]
            acc = acc + slab * cw_ref[w:w + 1, cols]
        return acc

    def mix(sub, proj_out):
        lo = sub * ts
        q = proj_out["q"]
        n_cols = ATTN_WIDTH // LANES
        units = [(c, b) for c in range(n_cols) for b in range(n_blocks)]
        conv = {}

        def conv_some(first_unit, count):
            for c, b in units[first_unit:first_unit + count]:
                conv[(c, b)] = conv_piece(sub, b, c)

        per_phase = -(-len(units) // 4)
        heads = [(c, b, e) for c, b in units for e in range(2)]
        kv_of = lambda c: c // (N_HEADS // N_KV_HEADS // 2)
        band_of = lambda b: pl.ds(lo + b * BLOCK, 2 * BLOCK)

        scores = {}
        for c, b, e in heads:
            q_aug = jnp.concatenate([q[c][b * BLOCK:(b + 1) * BLOCK], eye], axis=1)
            k_aug = jnp.concatenate([kv_ref[2 * kv_of(c) + e, band_of(b), :],
                                     first_bias_t if lo + b == 0 else bias_t], axis=1)
            scores[(c, b, e)] = lax.dot_general(q_aug, k_aug, nt, preferred_element_type=F32)
        conv_some(0, per_phase)
        yield

        row_max = {}
        for c, b, e in heads:
            row_max[(c, b, e)] = jnp.maximum(jnp.max(scores[(c, b, e)], axis=-1, keepdims=True),
                                             sink_ref[layer, 2 * c + e] * LOG2E)
        conv_some(per_phase, per_phase)
        yield

        probs, recip = {}, {}
        for c, b, e in heads:
            m = row_max[(c, b, e)]
            ex = jnp.exp2(scores[(c, b, e)] - m)
            denom = jnp.sum(ex, axis=-1, keepdims=True) + jnp.exp2(sink_ref[layer, 2 * c + e] * LOG2E - m)
            probs[(c, b, e)] = ex.astype(BF16)
            recip[(c, b, e)] = 1.0 / denom
        conv_some(2 * per_phase, per_phase)
        yield

        outs = {}
        for c, b in units:
            o = [jnp.dot(probs[(c, b, e)], kv_ref[4 + 2 * kv_of(c) + e, band_of(b), :],
                         preferred_element_type=F32) * recip[(c, b, e)] for e in range(2)]
            outs[(c, b)] = o[0] + o[1]
        conv_some(3 * per_phase, per_phase)
        yield

        gather = lambda d: jnp.concatenate(
            [jnp.concatenate([d[(c, b)] for b in range(n_blocks)], axis=0) for c in range(n_cols)], axis=1)
        attn_out = gather(outs).astype(BF16)
        acc = gather(conv)
        mu = jnp.mean(acc, axis=-1, keepdims=True)
        xc = acc - mu
        y = xc * lax.rsqrt(jnp.mean(xc * xc, axis=-1, keepdims=True) + EPS) * lng_ref[...] + lnb_ref[...]
        conv_out = (y * _sigmoid(y)).astype(BF16)
        mixed = jnp.concatenate([attn_out, conv_out], axis=1)
        o_ref[pl.ds(lo, ts), :] = proj_out["x"] + jnp.dot(mixed, wout_ref[...], preferred_element_type=F32)

    proj_out = {}
    for _ in project(0, proj_out):
        pass
    for sub in range(n_sub):
        next_out = {}
        stages = [mix(sub, proj_out)]
        if sub + 1 < n_sub:
            stages.append(project(sub + 1, next_out))
        while stages:
            for g in list(stages):
                if next(g, StopIteration) is StopIteration:
                    stages.remove(g)
        proj_out = next_out

    kv_ref[:, 0:BLOCK, :] = kv_ref[:, tm:tm + BLOCK, :]
    hg_ref[0:CONV_HIST, :] = hg_ref[tm:tm + CONV_HIST, :]


def _mixer(x, layer, cos, sin, norm_g, w_in, conv_w, conv_b, ln_g, ln_b, sinks, w_out, *, tm=1024, ts=256):
    B, S, d = x.shape
    d_in = w_in.shape[2]
    conv_c = conv_w.shape[2]
    d_mix = w_out.shape[1]
    assert conv_c == ATTN_WIDTH
    tile = lambda width: pl.BlockSpec((None, tm, width), lambda b, s: (b, s, 0))
    return pl.pallas_call(
        functools.partial(_mixer_kernel, tm=tm, ts=ts, layer=layer),
        grid=(B, S // tm),
        in_specs=[pl.BlockSpec(memory_space=pltpu.SMEM),
                  tile(d), tile(LANES), tile(LANES),
                  _layer_spec(layer, 1, d),
                  _layer_spec(0, d, d_in),
                  _layer_spec(layer, CONV_WIDTH, conv_c),
                  _layer_spec(layer, 1, conv_c),
                  _layer_spec(layer, 1, conv_c),
                  _layer_spec(layer, 1, conv_c),
                  _layer_spec(0, d_mix, d)],
        out_specs=tile(d),
        out_shape=jax.ShapeDtypeStruct((B, S, d), F32),
        scratch_shapes=[pltpu.VMEM((8, BLOCK + tm, LANES), BF16),
                        pltpu.VMEM((CONV_HIST + tm, conv_c), F32),
                        pltpu.VMEM((2, SUBLANES - 1, SUBLANES + CONV_HIST + ts, conv_c), F32)],
        compiler_params=pltpu.CompilerParams(
            dimension_semantics=("arbitrary", "arbitrary"), vmem_limit_bytes=VMEM_LIMIT_BYTES),
        name="mixer",
    )(sinks, x, cos, sin, norm_g, w_in, conv_w, conv_b, ln_g, ln_b, w_out)


def kernel(x, positions, ffn1_norm, ffn1_w_gate, ffn1_w_up, ffn1_w_down, mix_norm, w_in, conv_w, conv_b,
           conv_ln_g, conv_ln_b, attn_sinks, w_out, ffn2_norm, ffn2_w_gate, ffn2_w_up, ffn2_w_down,
           final_norm):
    B, S, d = x.shape
    depth = w_in.shape[0]
    row = lambda p: p.reshape(depth, 1, p.shape[-1])
    ffn1_f32 = (ffn1_w_gate, ffn1_w_up, ffn1_w_down)
    ffn2_f32 = (ffn2_w_gate, ffn2_w_up, ffn2_w_down)
    ffn1_g, ffn2_g = row(ffn1_norm), row(ffn2_norm)
    final_g = final_norm.reshape(1, d)
    cos, sin, weights = _rope_tables(positions, (ffn1_f32, 0))
    x = x.reshape(B * S, d)
    for l in range(depth):
        last = l == depth - 1
        x, cast = _ffn(x, 0, ffn1_g[l:l + 1], weights, final_g, final_norm=False,
                       cast_next=(ffn2_f32 + (w_in, w_out), l))
        weights, (w_in_l, w_out_l) = cast[:3], cast[3:]
        x = _mixer(x.reshape(B, S, d), l, cos, sin, row(mix_norm), w_in_l, conv_w, row(conv_b), row(conv_ln_g),
                   row(conv_ln_b), attn_sinks, w_out_l).reshape(B * S, d)
        x, weights = _ffn(x, 0, ffn2_g[l:l + 1], weights, final_g, final_norm=last,
                          cast_next=None if last else (ffn1_f32, l + 1))
    return x.reshape(B, S, d)
```

```python
import functools

import jax
import jax.numpy as jnp
from jax import lax
from jax.experimental import pallas as pl
from jax.experimental.pallas import tpu as pltpu

HEAD_DIM = 64
N_HEADS = 8
N_KV_HEADS = 2
ATTN_WIDTH = N_HEADS * HEAD_DIM
KV_WIDTH = N_KV_HEADS * HEAD_DIM
CONV_WIDTH = 31
BLOCK = 128
ROPE_THETA = 10000.0
EPS = 1e-5
LOG2E = 1.4426950408889634

LANES = 128
SUBLANES = 8
PACK = 16
MXU_COLS = 256
VMEM_LIMIT_BYTES = 56 * 1024 * 1024

CONV_HIST = 32
CONV_MXU_TAPS = tuple(range(0, CONV_WIDTH, 2))
HALF = HEAD_DIM // 2

BF16 = jnp.bfloat16
F32 = jnp.float32


def _sigmoid(x):
    return 1.0 / (1.0 + jnp.exp(-x))


def _rms_norm(x, g):
    return x * lax.rsqrt(jnp.mean(x * x, axis=-1, keepdims=True) + EPS) * g


def _layer_spec(layer, *shape):
    zeros = (0,) * len(shape)
    return pl.BlockSpec((None,) + shape, lambda *_: (layer,) + zeros, pipeline_mode=pl.Buffered(1))


def _cast_specs(cast_next, steps):
    arrays, layer = cast_next
    in_specs, out_specs, shapes = [], [], []
    for w in arrays:
        slab = w.shape[1] // steps
        assert slab * steps == w.shape[1] and slab % PACK == 0
        in_specs.append(pl.BlockSpec((None, slab, w.shape[2]), lambda i: (layer, i, 0)))
        out_specs.append(pl.BlockSpec((None, slab, w.shape[2]), lambda i: (0, i, 0)))
        shapes.append(jax.ShapeDtypeStruct((1,) + w.shape[1:], BF16))
    return in_specs, out_specs, shapes


def _rope_table_kernel(pos_ref, invf_ref, *refs):
    n_cast = (len(refs) - 2) // 2
    cos_ref, sin_ref = refs[n_cast:n_cast + 2]
    for src, dst in zip(refs[:n_cast], refs[n_cast + 2:]):
        dst[...] = src[...].astype(BF16)
    ang = pos_ref[...].astype(F32) * invf_ref[...]
    lane = lax.broadcasted_iota(jnp.int32, (1, LANES), 1)
    sign = jnp.where((lane & (HEAD_DIM - 1)) < HALF, -1.0, 1.0).astype(F32)
    for table, out_ref, scale in ((jnp.cos(ang), cos_ref, None), (jnp.sin(ang), sin_ref, sign)):
        for i in range(LANES // HALF):
            t = jnp.where((lane >= i * HALF) & (lane < (i + 1) * HALF), table, 0.0)
            t = t + pltpu.roll(t, 2 * HALF, 1)
            t = t + pltpu.roll(t, HALF, 1)
            out_ref[i] = t if scale is None else t * scale


def _rope_tables(positions, cast_next):
    B, S = positions.shape
    n = B * S
    groups = LANES // HALF
    rows = n // groups
    inv_freq = 1.0 / (ROPE_THETA ** (jnp.arange(0, HEAD_DIM, 2, dtype=F32) / HEAD_DIM))
    invf = jnp.tile(inv_freq, groups).reshape(1, LANES)
    pos = jnp.repeat(positions.reshape(groups, rows).T, HALF, axis=1)
    tr = 1024
    spec_out = pl.BlockSpec((groups, tr, LANES), lambda i: (0, i, 0))
    cast_in_specs, cast_out_specs, cast_shapes = _cast_specs(cast_next, rows // tr)
    cos, sin, *cast = pl.pallas_call(
        _rope_table_kernel,
        grid=(rows // tr,),
        in_specs=[pl.BlockSpec((tr, LANES), lambda i: (i, 0)),
                  pl.BlockSpec((1, LANES), lambda i: (0, 0))] + cast_in_specs,
        out_specs=[spec_out, spec_out] + cast_out_specs,
        out_shape=[jax.ShapeDtypeStruct((groups, rows, LANES), F32)] * 2 + cast_shapes,
        compiler_params=pltpu.CompilerParams(dimension_semantics=("arbitrary",),
                                             vmem_limit_bytes=VMEM_LIMIT_BYTES),
        name="rope_tables",
    )(pos, invf, *cast_next[0])
    return cos.reshape(B, S, LANES), sin.reshape(B, S, LANES), tuple(cast)


def _ffn_kernel(*refs, chunks, final_norm, parts, n_cast):
    x_ref, g_ref, wg_ref, wu_ref, wd_ref, fg_ref = refs[:6]
    cast_in = refs[6:6 + n_cast]
    o_ref = refs[6 + n_cast]
    cast_out = refs[7 + n_cast:7 + 2 * n_cast]
    a_ref = refs[7 + 2 * n_cast]
    tm = x_ref.shape[0]
    rows = tm // parts

    def hidden(h, lo, width):
        gate = jnp.dot(h, wg_ref[:, lo:lo + width], preferred_element_type=F32)
        up = jnp.dot(h, wu_ref[:, lo:lo + width], preferred_element_type=F32)
        return (gate * _sigmoid(gate) * up).astype(BF16)

    h_parts = []
    lo, width = chunks[0]
    for j in range(parts):
        part = pl.ds(j * rows, rows)
        h_parts.append(_rms_norm(x_ref[part, :], g_ref[...]).astype(BF16))
        a_ref[part, lo:lo + width] = hidden(h_parts[j], lo, width)
    h = jnp.concatenate(h_parts, axis=0)
    for i, (lo, width) in enumerate(chunks[1:]):
        a_ref[:, lo:lo + width] = hidden(h, lo, width)
        if i < n_cast:
            cast_out[i][...] = cast_in[i][...].astype(BF16)
    y = x_ref[...] + 0.5 * jnp.dot(a_ref[...], wd_ref[...], preferred_element_type=F32)
    if final_norm:
        y = _rms_norm(y, fg_ref[...])
    o_ref[...] = y


def _ff_chunks(d_ff, chunk):
    out, lo = [], 0
    while lo < d_ff:
        w = min(chunk, d_ff - lo)
        out.append((lo, w))
        lo += w
    return tuple(out)


def _ffn(x, layer, norm_g, weights, final_g, *, final_norm, cast_next=None, tm=1024, ff_chunk=256, parts=4):
    n, d = x.shape
    w_gate, w_up, w_down = weights
    d_ff = w_gate.shape[2]
    steps = n // tm
    cast_arrays = cast_next[0] if cast_next is not None else ()
    cast_in_specs, cast_out_specs, cast_shapes = _cast_specs(cast_next, steps) if cast_arrays else ([], [], [])
    assert len(cast_arrays) < len(_ff_chunks(d_ff, ff_chunk))
    outs = pl.pallas_call(
        functools.partial(_ffn_kernel, chunks=_ff_chunks(d_ff, ff_chunk), final_norm=final_norm, parts=parts,
                          n_cast=len(cast_arrays)),
        grid=(steps,),
        in_specs=[pl.BlockSpec((tm, d), lambda i: (i, 0)),
                  _layer_spec(layer, 1, d),
                  _layer_spec(layer, d, d_ff),
                  _layer_spec(layer, d, d_ff),
                  _layer_spec(layer, d_ff, d),
                  pl.BlockSpec((1, d), lambda i: (0, 0), pipeline_mode=pl.Buffered(1))] + cast_in_specs,
        out_specs=[pl.BlockSpec((tm, d), lambda i: (i, 0))] + cast_out_specs,
        out_shape=[jax.ShapeDtypeStruct((n, d), F32)] + cast_shapes,
        scratch_shapes=[pltpu.VMEM((tm, d_ff), BF16)],
        compiler_params=pltpu.CompilerParams(
            dimension_semantics=("arbitrary",), vmem_limit_bytes=VMEM_LIMIT_BYTES),
        name="ffn",
    )(x, norm_g, w_gate, w_up, w_down, final_g, *cast_arrays)
    return outs[0], tuple(outs[1:])


def _mixer_kernel(sink_ref, x_ref, cos_ref, sin_ref, g_ref, win_ref, cw_ref, cd_ref, cb_ref, lng_ref, lnb_ref,
                  wout_ref, o_ref, kv_ref, hg_ref, sh_ref, *, tm, ts, layer):
    s_idx = pl.program_id(1)
    conv_c = cw_ref.shape[1]
    n_sub = tm // ts
    n_blocks = ts // BLOCK
    q_end, k_end, v_end = ATTN_WIDTH, ATTN_WIDTH + KV_WIDTH, ATTN_WIDTH + 2 * KV_WIDTH

    @pl.when(s_idx == 0)
    def _():
        kv_ref[:, 0:BLOCK, :] = jnp.zeros((8, BLOCK, LANES), BF16)
        hg_ref[0:CONV_HIST, :] = jnp.zeros((CONV_HIST, conv_c), F32)

    lane = lax.broadcasted_iota(jnp.int32, (1, LANES), 1)
    low_half = (lane & (HEAD_DIM - 1)) < HALF
    first = lane < HEAD_DIM
    zero = jnp.zeros((), F32)

    qi = lax.broadcasted_iota(jnp.int32, (BLOCK, 2 * BLOCK), 0)
    kj = lax.broadcasted_iota(jnp.int32, (BLOCK, 2 * BLOCK), 1)
    in_window = (kj > qi) & (kj <= qi + BLOCK)
    first_block_mask = in_window & ((kj >= BLOCK) | (s_idx > 0))
    neg = jnp.finfo(F32).min
    nt = (((1,), (1,)), ((), ()))

    def project(sub, out):
        lo = sub * ts
        tile_rows = pl.ds(lo, ts)
        x = x_ref[tile_rows, :]
        out["x"] = x
        h = _rms_norm(x, g_ref[...]).astype(BF16)
        cos = cos_ref[tile_rows, :]
        sin = sin_ref[tile_rows, :]

        def proj(lo_col, width):
            return jnp.dot(h, win_ref[:, lo_col:lo_col + width], preferred_element_type=F32)

        def rope(t, cos, sin):
            partner = jnp.where(low_half, pltpu.roll(t, LANES - HALF, 1), pltpu.roll(t, HALF, 1))
            return t * cos + partner * sin

        cos_q, sin_q = cos * (HEAD_DIM ** -0.5 * LOG2E), sin * (HEAD_DIM ** -0.5 * LOG2E)
        out["q"] = []
        for lo_col in range(0, q_end, MXU_COLS):
            pq = proj(lo_col, MXU_COLS)
            out["q"] += [rope(pq[:, i * LANES:(i + 1) * LANES], cos_q, sin_q).astype(BF16)
                         for i in range(MXU_COLS // LANES)]
            yield

        pkv = proj(q_end, 2 * KV_WIDTH)
        k = rope(pkv[:, :KV_WIDTH], cos, sin)
        v = pkv[:, KV_WIDTH:]
        rows = pl.ds(BLOCK + lo, ts)
        k_sw = pltpu.roll(k, HEAD_DIM, 1)
        v_sw = pltpu.roll(v, HEAD_DIM, 1)
        for i, t in enumerate((jnp.where(first, k, zero), jnp.where(first, zero, k_sw),
                               jnp.where(first, k_sw, zero), jnp.where(first, zero, k),
                               jnp.where(first, v, zero), jnp.where(first, zero, v_sw),
                               jnp.where(first, v_sw, zero), jnp.where(first, zero, v))):
            kv_ref[i, rows, :] = t.astype(BF16)
        yield

        n_rows = CONV_HIST + ts
        for lo_col in range(0, conv_c, MXU_COLS):
            a = proj(v_end + lo_col, MXU_COLS)
            yield
            gate = proj(v_end + conv_c + lo_col, MXU_COLS)
            cols = pl.ds(lo_col, MXU_COLS)
            hg_ref[pl.ds(CONV_HIST + lo, ts), cols] = a * _sigmoid(gate)
            hbuf = hg_ref[pl.ds(lo, n_rows), cols]
            for r in range(1, SUBLANES):
                sh_ref[sub % 2, r - 1, pl.ds(SUBLANES - r, n_rows), cols] = hbuf
            yield

    pair_cache = {}

    def conv_piece(sub, b, c):
        if c % 2 == 1:
            return pair_cache.pop((sub, b, c))
        lo = sub * ts
        cols = pl.ds(c * LANES, MXU_COLS)
        acc = jnp.broadcast_to(cb_ref[:, cols], (BLOCK, MXU_COLS))
        on_mxu = []
        for w in range(CONV_WIDTH):
            base, r = divmod(CONV_HIST - (CONV_WIDTH - 1) + w, SUBLANES)
            if r == 0:
                slab = hg_ref[pl.ds(lo + base * SUBLANES + b * BLOCK, BLOCK), cols]
            else:
                slab = sh_ref[sub % 2, r - 1, pl.ds((base + 1) * SUBLANES + b * BLOCK, BLOCK), cols]
            if w in CONV_MXU_TAPS:
                on_mxu.append(slab.astype(BF16))
            else:
                acc = acc + slab * cw_ref[w:w + 1, cols]
        acc = acc + jnp.dot(jnp.concatenate(on_mxu, axis=1), cd_ref[c // 2], preferred_element_type=F32)
        pair_cache[(sub, b, c + 1)] = acc[:, LANES:]
        return acc[:, :LANES]

    def mix(sub, proj_out):
        lo = sub * ts
        q = proj_out["q"]
        n_cols = ATTN_WIDTH // LANES
        units = [(c, b) for c in range(n_cols) for b in range(n_blocks)]
        conv = {}

        def conv_some(first_unit, count):
            for c, b in units[first_unit:first_unit + count]:
                conv[(c, b)] = conv_piece(sub, b, c)

        per_phase = -(-len(units) // 4)
        heads = [(c, b, e) for c, b in units for e in range(2)]
        kv_of = lambda c: c // (N_HEADS // N_KV_HEADS // 2)
        band_of = lambda b: pl.ds(lo + b * BLOCK, 2 * BLOCK)

        scores = {}
        for c, b, e in heads:
            qb = q[c][b * BLOCK:(b + 1) * BLOCK]
            scores[(c, b, e)] = lax.dot_general(qb, kv_ref[2 * kv_of(c) + e, band_of(b), :], nt,
                                                preferred_element_type=F32)
        conv_some(0, per_phase)
        yield

        row_max = {}
        for c, b, e in heads:
            mask = first_block_mask if lo + b == 0 else in_window
            scores[(c, b, e)] = jnp.where(mask, scores[(c, b, e)], neg)
            row_max[(c, b, e)] = jnp.maximum(jnp.max(scores[(c, b, e)], axis=-1, keepdims=True),
                                             sink_ref[layer, 2 * c + e] * LOG2E)
        conv_some(per_phase, per_phase)
        yield

        probs, recip = {}, {}
        for c, b, e in heads:
            m = row_max[(c, b, e)]
            ex = jnp.exp2(scores[(c, b, e)] - m)
            denom = jnp.sum(ex, axis=-1, keepdims=True) + jnp.exp2(sink_ref[layer, 2 * c + e] * LOG2E - m)
            probs[(c, b, e)] = ex.astype(BF16)
            recip[(c, b, e)] = 1.0 / denom
        conv_some(2 * per_phase, per_phase)
        yield

        outs = {}
        for c, b in units:
            o = [jnp.dot(probs[(c, b, e)], kv_ref[4 + 2 * kv_of(c) + e, band_of(b), :],
                         preferred_element_type=F32) * recip[(c, b, e)] for e in range(2)]
            outs[(c, b)] = o[0] + o[1]
        conv_some(3 * per_phase, per_phase)
        yield

        gather = lambda d: jnp.concatenate(
            [jnp.concatenate([d[(c, b)] for b in range(n_blocks)], axis=0) for c in range(n_cols)], axis=1)
        attn_out = gather(outs).astype(BF16)
        acc = gather(conv)
        mu = jnp.mean(acc, axis=-1, keepdims=True)
        xc = acc - mu
        y = xc * lax.rsqrt(jnp.mean(xc * xc, axis=-1, keepdims=True) + EPS) * lng_ref[...] + lnb_ref[...]
        conv_out = (y * _sigmoid(y)).astype(BF16)
        mixed = jnp.concatenate([attn_out, conv_out], axis=1)
        o_ref[pl.ds(lo, ts), :] = proj_out["x"] + jnp.dot(mixed, wout_ref[...], preferred_element_type=F32)

    proj_out = {}
    for _ in project(0, proj_out):
        pass
    for sub in range(n_sub):
        next_out = {}
        stages = [mix(sub, proj_out)]
        if sub + 1 < n_sub:
            stages.append(project(sub + 1, next_out))
        while stages:
            for g in list(stages):
                if next(g, StopIteration) is StopIteration:
                    stages.remove(g)
        proj_out = next_out

    kv_ref[:, 0:BLOCK, :] = kv_ref[:, tm:tm + BLOCK, :]
    hg_ref[0:CONV_HIST, :] = hg_ref[tm:tm + CONV_HIST, :]


def _conv_diag(conv_w):
    depth, _, c = conv_w.shape
    t = len(CONV_MXU_TAPS)
    w = conv_w[:, jnp.array(CONV_MXU_TAPS), :].reshape(depth, t, c // MXU_COLS, MXU_COLS).transpose(0, 2, 1, 3)
    diag = w[..., None] * jnp.eye(MXU_COLS, dtype=F32)
    return diag.reshape(depth, c // MXU_COLS, t * MXU_COLS, MXU_COLS).astype(BF16)


def _mixer(x, layer, cos, sin, norm_g, w_in, conv_w, conv_d, conv_b, ln_g, ln_b, sinks, w_out, *, tm=1024,
           ts=256):
    B, S, d = x.shape
    d_in = w_in.shape[2]
    conv_c = conv_w.shape[2]
    d_mix = w_out.shape[1]
    assert conv_c == ATTN_WIDTH
    tile = lambda width: pl.BlockSpec((None, tm, width), lambda b, s: (b, s, 0))
    return pl.pallas_call(
        functools.partial(_mixer_kernel, tm=tm, ts=ts, layer=layer),
        grid=(B, S // tm),
        in_specs=[pl.BlockSpec(memory_space=pltpu.SMEM),
                  tile(d), tile(LANES), tile(LANES),
                  _layer_spec(layer, 1, d),
                  _layer_spec(0, d, d_in),
                  _layer_spec(layer, CONV_WIDTH, conv_c),
                  _layer_spec(layer, *conv_d.shape[1:]),
                  _layer_spec(layer, 1, conv_c),
                  _layer_spec(layer, 1, conv_c),
                  _layer_spec(layer, 1, conv_c),
                  _layer_spec(0, d_mix, d)],
        out_specs=tile(d),
        out_shape=jax.ShapeDtypeStruct((B, S, d), F32),
        scratch_shapes=[pltpu.VMEM((8, BLOCK + tm, LANES), BF16),
                        pltpu.VMEM((CONV_HIST + tm, conv_c), F32),
                        pltpu.VMEM((2, SUBLANES - 1, SUBLANES + CONV_HIST + ts, conv_c), F32)],
        compiler_params=pltpu.CompilerParams(
            dimension_semantics=("arbitrary", "arbitrary"), vmem_limit_bytes=VMEM_LIMIT_BYTES),
        name="mixer",
    )(sinks, x, cos, sin, norm_g, w_in, conv_w, conv_d, conv_b, ln_g, ln_b, w_out)


def kernel(x, positions, ffn1_norm, ffn1_w_gate, ffn1_w_up, ffn1_w_down, mix_norm, w_in, conv_w, conv_b,
           conv_ln_g, conv_ln_b, attn_sinks, w_out, ffn2_norm, ffn2_w_gate, ffn2_w_up, ffn2_w_down,
           final_norm):
    B, S, d = x.shape
    depth = w_in.shape[0]
    row = lambda p: p.reshape(depth, 1, p.shape[-1])
    ffn1_f32 = (ffn1_w_gate, ffn1_w_up, ffn1_w_down)
    ffn2_f32 = (ffn2_w_gate, ffn2_w_up, ffn2_w_down)
    ffn1_g, ffn2_g = row(ffn1_norm), row(ffn2_norm)
    final_g = final_norm.reshape(1, d)
    conv_d = _conv_diag(conv_w)
    cos, sin, weights = _rope_tables(positions, (ffn1_f32, 0))
    x = x.reshape(B * S, d)
    for l in range(depth):
        last = l == depth - 1
        x, cast = _ffn(x, 0, ffn1_g[l:l + 1], weights, final_g, final_norm=False,
                       cast_next=(ffn2_f32 + (w_in, w_out), l))
        weights, (w_in_l, w_out_l) = cast[:3], cast[3:]
        x = _mixer(x.reshape(B, S, d), l, cos, sin, row(mix_norm), w_in_l, conv_w, conv_d, row(conv_b), row(conv_ln_g),
                   row(conv_ln_b), attn_sinks, w_out_l).reshape(B * S, d)
        x, weights = _ffn(x, 0, ffn2_g[l:l + 1], weights, final_g, final_norm=last,
                          cast_next=None if last else (ffn1_f32, l + 1))
    return x.reshape(B, S, d)
```
